```python
import jax
import jax.numpy as jnp
from jax import lax
import numpy as np

D_MODEL = 2048
BATCH = 4
SEQ = 2048
DEPTH = 1
DEC_BATCH = 128
DEC_SEQ = 4
PAST_LEN = 16384
PAGE_SIZE = 128

N_META = 16
HEAD_DIM = 128
H_A = D_MODEL // (2 * HEAD_DIM)
Q_LORA = D_MODEL // 4
KV_LORA = 512
NOPE = HEAD_DIM
ROPE = 64
DV_A = HEAD_DIM
ROPE_THETA = 10000.0
SM_SCALE = (NOPE + ROPE) ** -0.5
Q_BLOCK = 128
H_B = D_MODEL // (2 * HEAD_DIM)
DK = 128
DV = HEAD_DIM
CHUNK = 64
D_MIX = H_A * DV_A + H_B * DV
IN_SPLITS = (Q_LORA, KV_LORA, ROPE, H_B * DK, H_B * DK, H_B * DV, H_B * DV)
D_IN = sum(IN_SPLITS)
N_GROUPS = 4
EXP_PER_GROUP = 8
N_EXPERTS = N_GROUPS * EXP_PER_GROUP
TOP_K = 2
D_FF = D_MODEL // 4
EPS = 1e-6

kernel_name = 'hymba_mla_hgrn2_hmoe_step'


def rmsnorm(x, w):
    xf = x.astype(jnp.float32)
    y = xf * lax.rsqrt(jnp.mean(xf * xf, axis=-1, keepdims=True) + EPS)
    return (y * w.astype(jnp.float32)).astype(x.dtype)


def rope(x, pos):
    half = x.shape[-1] // 2
    inv_freq = jnp.power(ROPE_THETA, -jnp.arange(half, dtype=jnp.float32) / half)
    ang = pos.astype(jnp.float32)[:, None] * inv_freq[None, :]
    cos = jnp.cos(ang)[None, :, None, :]
    sin = jnp.sin(ang)[None, :, None, :]
    x1 = x[..., :half].astype(jnp.float32)
    x2 = x[..., half:].astype(jnp.float32)
    return jnp.concatenate([x1 * cos - x2 * sin, x1 * sin + x2 * cos], axis=-1).astype(x.dtype)


def split_in(z):
    return jnp.split(z, np.cumsum(IN_SPLITS)[:-1].tolist(), axis=-1)


def hgrn_gates(qb, fb, ib, lb):
    n, L = qb.shape[:2]
    shp = (n, L, H_B, DK)
    fb = fb.astype(jnp.float32)
    q = jax.nn.silu(qb.astype(jnp.float32)).reshape(shp)
    logf = jnp.log(lb + (1.0 - lb) * jax.nn.sigmoid(fb)).reshape(shp)
    k = ((1.0 - lb) * jax.nn.sigmoid(-fb)).reshape(shp)
    v = ib.astype(jnp.float32).reshape(n, L, H_B, DV)
    return q, k, logf, v


def hgrn_chunk(S0, q, k, logf, v):
    L = q.shape[1]
    b = jnp.cumsum(logf, axis=1)
    causal = jnp.tril(jnp.ones((L, L), dtype=bool))[None, :, :, None, None]
    decay = jnp.exp(jnp.where(causal, b[:, :, None] - b[:, None, :], -jnp.inf))
    a = jnp.einsum('ntshk,nthk,nshk->nhts', decay, q, k)
    o = jnp.einsum('nhts,nshv->nthv', a, v) + jnp.einsum('nthk,nhkv->nthv', q * jnp.exp(b), S0)
    b_end = b[:, -1]
    w = k * jnp.exp(b_end[:, None] - b)
    S1 = jnp.exp(b_end)[..., None] * S0 + jnp.einsum('nshk,nshv->nhkv', w, v)
    return o, S1


def hgrn_seq(S0, q, k, logf, v):
    n, L = q.shape[:2]
    n_full = L // CHUNK
    rem = L - n_full * CHUNK
    outs = []
    S = S0
    if n_full > 0:
        def to_chunks(a):
            return jnp.moveaxis(a[:, :n_full * CHUNK].reshape(n, n_full, CHUNK, *a.shape[2:]), 1, 0)

        def step(S_c, xs):
            o_c, S_n = hgrn_chunk(S_c, *xs)
            return S_n, o_c

        S, o_f = lax.scan(step, S, (to_chunks(q), to_chunks(k), to_chunks(logf), to_chunks(v)))
        outs.append(jnp.moveaxis(o_f, 0, 1).reshape(n, n_full * CHUNK, H_B, DV))
    if rem > 0:
        s0 = n_full * CHUNK
        o_r, S = hgrn_chunk(S, q[:, s0:], k[:, s0:], logf[:, s0:], v[:, s0:])
        outs.append(o_r)
    o = outs[0] if len(outs) == 1 else jnp.concatenate(outs, axis=1)
    return o, S


def hgrn_readout(o, gb, w):
    n, L = o.shape[:2]
    on = o * lax.rsqrt(jnp.mean(o * o, axis=-1, keepdims=True) + EPS)
    on = on * w.astype(jnp.float32).reshape(H_B, DV)
    return (on.reshape(n, L, H_B * DV) * jax.nn.silu(gb.astype(jnp.float32))).astype(gb.dtype)


def mixer_inputs(h, pos, lb, attn_norm_w, w_in, q_norm_w, w_uq, kv_norm_w, w_uk):
    z = rmsnorm(h, attn_norm_w) @ w_in
    cq, ckv_raw, kr_raw, qb, fb, ib, gb = split_in(z)
    q = jnp.einsum('nlc,chd->nlhd', rmsnorm(cq, q_norm_w), w_uq)
    q_rope = rope(q[..., NOPE:], pos)
    q_lat = jnp.einsum('nlhd,chd->nlhc', q[..., :NOPE], w_uk)
    ckv = rmsnorm(ckv_raw, kv_norm_w)
    kr = rope(kr_raw[:, :, None, :], pos)[:, :, 0, :]
    hg = hgrn_gates(qb, fb, ib, lb)
    return (q_lat, q_rope, ckv, kr), hg, gb


def mla_prompt(q_lat, q_rope, ckv, kr):
    n, T = q_lat.shape[:2]
    k_pos = jnp.arange(T)

    def attend(ql, qr, q_pos, ck, kk, kp):
        s = (jnp.einsum('nqhc,nkc->nhqk', ql, ck) + jnp.einsum('nqhr,nkr->nhqk', qr, kk)).astype(jnp.float32) * SM_SCALE
        s = jnp.where(kp[None, :] <= q_pos[:, None], s, -jnp.inf)
        p = jax.nn.softmax(s, axis=-1).astype(ck.dtype)
        return jnp.einsum('nhqk,nkc->nqhc', p, ck)

    o_meta = attend(q_lat[:, :N_META], q_rope[:, :N_META], k_pos[:N_META], ckv[:, :N_META], kr[:, :N_META], k_pos[:N_META])
    n_blk = (T - N_META) // Q_BLOCK

    def blocks(a):
        return jnp.moveaxis(a[:, N_META:].reshape(n, n_blk, Q_BLOCK, *a.shape[2:]), 1, 0)

    q_pos_blk = k_pos[N_META:].reshape(n_blk, Q_BLOCK)
    o_real = lax.map(lambda xs: attend(xs[0], xs[1], xs[2], ckv, kr, k_pos), (blocks(q_lat), blocks(q_rope), q_pos_blk))
    o_real = jnp.moveaxis(o_real, 0, 1).reshape(n, T - N_META, H_A, KV_LORA)
    return jnp.concatenate([o_meta, o_real], axis=1)


def mla_sample(q_lat, q_rope, ckv_new, kr_new, cache_ckv_l, cache_kr_l, page_table):
    S = q_lat.shape[1]
    causal = jnp.tril(jnp.ones((S, S), dtype=bool))[None]

    def one(xs):
        ql, qr, cn, kn, pt = xs
        cp = cache_ckv_l[pt].reshape(-1, KV_LORA)
        kp = cache_kr_l[pt].reshape(-1, ROPE)
        P = cp.shape[0]
        s_past = jnp.einsum('shc,kc->hsk', ql, cp) + jnp.einsum('shr,kr->hsk', qr, kp)
        s_new = jnp.einsum('shc,kc->hsk', ql, cn) + jnp.einsum('shr,kr->hsk', qr, kn)
        s_new = jnp.where(causal, s_new.astype(jnp.float32), -jnp.inf)
        s = jnp.concatenate([s_past.astype(jnp.float32), s_new], axis=-1) * SM_SCALE
        p = jax.nn.softmax(s, axis=-1).astype(cp.dtype)
        return jnp.einsum('hsk,kc->shc', p[..., :P], cp) + jnp.einsum('hsk,kc->shc', p[..., P:], cn)

    return lax.map(one, (q_lat, q_rope, ckv_new, kr_new, page_table))


def mixer_output(o_lat, o_rec, gb, w_uv, mla_out_norm_w, hgrn_out_norm_w, w_out):
    n, L = o_lat.shape[:2]
    y_a = rmsnorm(jnp.einsum('nlhc,chv->nlhv', o_lat, w_uv).reshape(n, L, H_A * DV_A), mla_out_norm_w)
    y_b = hgrn_readout(o_rec, gb, hgrn_out_norm_w).astype(y_a.dtype)
    return jnp.concatenate([y_a, y_b], axis=-1) @ w_out


def hier_moe(x, ffn_norm_w, w_rg, b_rg, w_re, b_re, w_gate, w_up, w_down):
    n, L, D = x.shape
    xf = rmsnorm(x, ffn_norm_w).reshape(n * L, D)
    pg = jax.nn.softmax((xf @ w_rg).astype(jnp.float32) + b_rg.astype(jnp.float32), axis=-1)
    pg_top, g_idx = lax.top_k(pg, 1)
    le = ((xf @ w_re).astype(jnp.float32) + b_re.astype(jnp.float32)).reshape(-1, N_GROUPS, EXP_PER_GROUP)
    le_g = jnp.take_along_axis(le, g_idx[:, :, None], axis=1)[:, 0]
    e_top, e_idx = lax.top_k(le_g, TOP_K)
    w_sel = pg_top * jax.nn.softmax(e_top, axis=-1)
    gates = jnp.sum(jax.nn.one_hot(g_idx * EXP_PER_GROUP + e_idx, N_EXPERTS, dtype=jnp.float32) * w_sel[..., None], axis=1)

    def add_expert(acc, ws):
        wg, wu, wd, ge = ws
        hdn = jax.nn.silu(xf @ wg) * (xf @ wu)
        return acc + (hdn @ wd).astype(jnp.float32) * ge[:, None], None

    acc, _ = lax.scan(add_expert, jnp.zeros((n * L, D), jnp.float32), (w_gate, w_up, w_down, gates.T))
    return acc.astype(x.dtype).reshape(n, L, D)


def setup_inputs(seed: int = 0) -> dict:
    key = jax.random.key(seed)
    ks = iter(jax.random.split(key, 32))
    n_pages = PAST_LEN // PAGE_SIZE
    n_used = DEC_BATCH * n_pages
    n_pool = n_used + n_used // 4

    def nrm(shape, scale):
        return jax.random.normal(next(ks), shape, jnp.float32) * scale

    def gain(shape):
        return 1.0 + 0.02 * jax.random.normal(next(ks), shape, jnp.float32)

    x_prompt = nrm((BATCH, SEQ, D_MODEL), 1.0)
    x_sample = nrm((DEC_BATCH, DEC_SEQ, D_MODEL), 1.0)
    cache_ckv = nrm((DEPTH, n_pool, PAGE_SIZE, KV_LORA), 1.0)
    cache_krope = nrm((DEPTH, n_pool, PAGE_SIZE, ROPE), 1.0)
    state_hgrn = nrm((DEPTH, DEC_BATCH, H_B, DK, DV), 0.5)
    page_table = jax.random.permutation(next(ks), n_pool)[:n_used].reshape(DEC_BATCH, n_pages).astype(jnp.int32)
    return {
        'x_prompt': x_prompt,
        'x_sample': x_sample,
        'cache_ckv': cache_ckv,
        'cache_krope': cache_krope,
        'state_hgrn': state_hgrn,
        'page_table': page_table,
        'meta_tokens': nrm((N_META, D_MODEL), 1.0),
        'attn_norm_w': gain((DEPTH, D_MODEL)),
        'w_in': nrm((DEPTH, D_MODEL, D_IN), D_MODEL ** -0.5),
        'q_norm_w': gain((DEPTH, Q_LORA)),
        'w_uq': nrm((DEPTH, Q_LORA, H_A, NOPE + ROPE), Q_LORA ** -0.5),
        'kv_norm_w': gain((DEPTH, KV_LORA)),
        'w_uk': nrm((DEPTH, KV_LORA, H_A, NOPE), KV_LORA ** -0.5),
        'w_uv': nrm((DEPTH, KV_LORA, H_A, DV_A), KV_LORA ** -0.5),
        'mla_out_norm_w': gain((DEPTH, H_A * DV_A)),
        'hgrn_lb_logits': nrm((DEPTH + 1, H_B * DK), 0.1),
        'hgrn_out_norm_w': gain((DEPTH, H_B * DV)),
        'w_out': nrm((DEPTH, D_MIX, D_MODEL), D_MIX ** -0.5),
        'ffn_norm_w': gain((DEPTH, D_MODEL)),
        'w_router_group': nrm((DEPTH, D_MODEL, N_GROUPS), D_MODEL ** -0.5),
        'b_router_group': nrm((DEPTH, N_GROUPS), 0.01),
        'w_router_expert': nrm((DEPTH, D_MODEL, N_EXPERTS), D_MODEL ** -0.5),
        'b_router_expert': nrm((DEPTH, N_EXPERTS), 0.01),
        'w_gate': nrm((DEPTH, N_EXPERTS, D_MODEL, D_FF), D_MODEL ** -0.5),
        'w_up': nrm((DEPTH, N_EXPERTS, D_MODEL, D_FF), D_MODEL ** -0.5),
        'w_down': nrm((DEPTH, N_EXPERTS, D_FF, D_MODEL), D_FF ** -0.5),
        'final_norm_w': gain((D_MODEL,)),
    }


def reference(x_prompt, x_sample, cache_ckv, cache_krope, state_hgrn, page_table, meta_tokens,
              attn_norm_w, w_in, q_norm_w, w_uq, kv_norm_w, w_uk, w_uv, mla_out_norm_w,
              hgrn_lb_logits, hgrn_out_norm_w, w_out, ffn_norm_w, w_router_group, b_router_group,
              w_router_expert, b_router_expert, w_gate, w_up, w_down, final_norm_w):
    n_p = x_prompt.shape[0]
    h_p = jnp.concatenate([jnp.broadcast_to(meta_tokens.astype(x_prompt.dtype)[None], (n_p, N_META, D_MODEL)), x_prompt], axis=1)
    h_s = x_sample
    T = h_p.shape[1]
    past = page_table.shape[1] * cache_ckv.shape[2]
    pos_p = jnp.arange(T)
    pos_s = past + jnp.arange(x_sample.shape[1])
    lower_bounds = jnp.cumsum(jax.nn.softmax(hgrn_lb_logits.astype(jnp.float32), axis=0), axis=0)
    ckv_p, kr_p, st_p, ckv_s, kr_s, st_s = [], [], [], [], [], []
    for l in range(DEPTH):
        lw_in = (lower_bounds[l], attn_norm_w[l], w_in[l], q_norm_w[l], w_uq[l], kv_norm_w[l], w_uk[l])
        lw_out = (w_uv[l], mla_out_norm_w[l], hgrn_out_norm_w[l], w_out[l])
        lw_ffn = (ffn_norm_w[l], w_router_group[l], b_router_group[l], w_router_expert[l], b_router_expert[l], w_gate[l], w_up[l], w_down[l])
        (ql, qr, ckv, kr), (qb, kb, lf, vb), gb = mixer_inputs(h_p, pos_p, *lw_in)
        o_lat = mla_prompt(ql, qr, ckv, kr)
        S0 = jnp.zeros((n_p, H_B, DK, DV), jnp.float32)
        o_m, S = hgrn_chunk(S0, qb[:, :N_META], kb[:, :N_META], lf[:, :N_META], vb[:, :N_META])
        o_r, S = hgrn_seq(S, qb[:, N_META:], kb[:, N_META:], lf[:, N_META:], vb[:, N_META:])
        o_rec = jnp.concatenate([o_m, o_r], axis=1)
        h_p = h_p + mixer_output(o_lat, o_rec, gb, *lw_out)
        h_p = h_p + hier_moe(h_p, *lw_ffn)
        ckv_p.append(ckv)
        kr_p.append(kr)
        st_p.append(S.astype(x_prompt.dtype))
        (ql, qr, ckv, kr), (qb, kb, lf, vb), gb = mixer_inputs(h_s, pos_s, *lw_in)
        o_lat = mla_sample(ql, qr, ckv, kr, cache_ckv[l], cache_krope[l], page_table)
        o_rec, S = hgrn_seq(state_hgrn[l].astype(jnp.float32), qb, kb, lf, vb)
        h_s = h_s + mixer_output(o_lat, o_rec, gb, *lw_out)
        h_s = h_s + hier_moe(h_s, *lw_ffn)
        ckv_s.append(ckv)
        kr_s.append(kr)
        st_s.append(S.astype(state_hgrn.dtype))
    y_prompt = rmsnorm(h_p, final_norm_w)[:, N_META:]
    y_sample = rmsnorm(h_s, final_norm_w)
    return (y_prompt, y_sample, jnp.stack(ckv_p), jnp.stack(kr_p), jnp.stack(st_p), jnp.stack(ckv_s), jnp.stack(kr_s), jnp.stack(st_s))
```

```python
import functools

import jax
import jax.numpy as jnp
from jax import lax
from jax.experimental import pallas as pl
from jax.experimental.pallas import tpu as pltpu

F32 = jnp.float32
BF16 = jnp.bfloat16

D_MODEL = 2048
N_META = 16
N_HEADS = 8
HEAD_DIM = 128
ROPE = 64
Q_LORA = 512
KV_LORA = 512
D_HALF = N_HEADS * HEAD_DIM
ROPE_THETA = 10000.0
SM_SCALE = (HEAD_DIM + ROPE) ** -0.5
EPS = 1e-6
N_GROUPS = 4
EXP_PER_GROUP = 8
N_EXPERTS = N_GROUPS * EXP_PER_GROUP
D_FF = 512

Z_WIDTH = 5376
Z_TILE = 1792
Z_ROPE_BLOCK = 40

LANES = 128
VMEM_LIMIT = 56 * 1024 * 1024
NEG = -1e30
HGRN_SAFE_DECAY = 60.0
MOE_TILE = 256
PAGES_PER_STEP = 16


def _nn(a, b):
    return jnp.dot(a, b, preferred_element_type=F32)


def _nt(a, b):
    return lax.dot_general(a, b, (((1,), (1,)), ((), ())), preferred_element_type=F32)


def _tn(a, b):
    return lax.dot_general(a, b, (((0,), (0,)), ((), ())), preferred_element_type=F32)


def _split3(x):
    hi = x.astype(BF16)
    r1 = x - hi.astype(F32)
    mid = r1.astype(BF16)
    lo = (r1 - mid.astype(F32)).astype(BF16)
    return hi, mid, lo


def _rms(x):
    return x * lax.rsqrt(jnp.mean(x * x, axis=-1, keepdims=True) + EPS)


def _silu(x):
    return x * jax.nn.sigmoid(x)


def _pick_tile(n, pref):
    if n <= pref:
        return n
    for t in range(pref, 7, -1):
        if n % t == 0 and t % 8 == 0:
            return t
    raise ValueError(f"no tile for {n}")


def _params(*sem):
    return pltpu.CompilerParams(dimension_semantics=sem, vmem_limit_bytes=VMEM_LIMIT)


def _inproj_kernel(x_ref, nw_ref, w_ref, z_ref, xn_ref):
    @pl.when(pl.program_id(1) == 0)
    def _():
        xn_ref[...] = (_rms(x_ref[...]) * nw_ref[...]).astype(BF16)

    z_ref[...] = _nn(xn_ref[...], w_ref[...])


def _in_proj(x, norm_w, w_z):
    n = x.shape[0]
    tm = _pick_tile(n, 512)
    return pl.pallas_call(
        _inproj_kernel,
        grid=(n // tm, Z_WIDTH // Z_TILE),
        in_specs=[
            pl.BlockSpec((tm, D_MODEL), lambda i, j: (i, 0)),
            pl.BlockSpec((1, D_MODEL), lambda i, j: (0, 0)),
            pl.BlockSpec((D_MODEL, Z_TILE), lambda i, j: (0, j)),
        ],
        out_specs=pl.BlockSpec((tm, Z_TILE), lambda i, j: (i, j)),
        out_shape=jax.ShapeDtypeStruct((n, Z_WIDTH), F32),
        scratch_shapes=[pltpu.VMEM((tm, D_MODEL), BF16)],
        compiler_params=_params("parallel", "arbitrary"),
        name="in_proj",
    )(x, norm_w, w_z)


def _prep_kernel(zq_ref, zkv_ref, zkr_ref, cs_ref, qnw_ref, kvnw_ref, wq_ref, wuk_ref,
                 ql_ref, qr_ref, ckv_ref, kr_ref, ckvb_ref, krb_ref):
    cqn = (_rms(zq_ref[...]) * qnw_ref[...]).astype(BF16)
    q = _nn(cqn, wq_ref[...])
    cs = cs_ref[...]
    cos = cs[:, :ROPE]
    sin = cs[:, ROPE:]
    for h in range(N_HEADS):
        qn = q[:, h * HEAD_DIM:(h + 1) * HEAD_DIM].astype(BF16)
        ql_ref[h] = (_nn(qn, wuk_ref[h]) * SM_SCALE).astype(BF16)
        a = q[:, D_HALF + h * ROPE:D_HALF + (h + 1) * ROPE]
        b = q[:, D_HALF + N_HEADS * ROPE + h * ROPE:D_HALF + N_HEADS * ROPE + (h + 1) * ROPE]
        qr_ref[h] = ((a * cos + b * sin) * SM_SCALE).astype(BF16)
    ckv = _rms(zkv_ref[...]) * kvnw_ref[...]
    ckv_ref[...] = ckv
    ckvb_ref[...] = ckv.astype(BF16)
    prod = zkr_ref[...] * cs
    kr = prod[:, :ROPE] + prod[:, ROPE:]
    kr_ref[...] = kr
    krb_ref[...] = kr.astype(BF16)


def _qkv_prep(z, cs, q_norm_w, kv_norm_w, w_q, w_ukt):
    n = z.shape[0]
    tm = _pick_tile(n, 256)
    const2 = lambda i: (0, 0)
    return pl.pallas_call(
        _prep_kernel,
        grid=(n // tm,),
        in_specs=[
            pl.BlockSpec((tm, Q_LORA), lambda i: (i, 0)),
            pl.BlockSpec((tm, KV_LORA), lambda i: (i, 1)),
            pl.BlockSpec((tm, LANES), lambda i: (i, Z_ROPE_BLOCK)),
            pl.BlockSpec((tm, LANES), lambda i: (i, 0)),
            pl.BlockSpec((1, Q_LORA), const2),
            pl.BlockSpec((1, KV_LORA), const2),
            pl.BlockSpec((Q_LORA, 2 * D_HALF), const2),
            pl.BlockSpec((N_HEADS, HEAD_DIM, KV_LORA), lambda i: (0, 0, 0)),
        ],
        out_specs=[
            pl.BlockSpec((N_HEADS, tm, KV_LORA), lambda i: (0, i, 0)),
            pl.BlockSpec((N_HEADS, tm, ROPE), lambda i: (0, i, 0)),
            pl.BlockSpec((tm, KV_LORA), lambda i: (i, 0)),
            pl.BlockSpec((tm, ROPE), lambda i: (i, 0)),
            pl.BlockSpec((tm, KV_LORA), lambda i: (i, 0)),
            pl.BlockSpec((tm, ROPE), lambda i: (i, 0)),
        ],
        out_shape=[
            jax.ShapeDtypeStruct((N_HEADS, n, KV_LORA), BF16),
            jax.ShapeDtypeStruct((N_HEADS, n, ROPE), BF16),
            jax.ShapeDtypeStruct((n, KV_LORA), F32),
            jax.ShapeDtypeStruct((n, ROPE), F32),
            jax.ShapeDtypeStruct((n, KV_LORA), BF16),
            jax.ShapeDtypeStruct((n, ROPE), BF16),
        ],
        compiler_params=_params("parallel"),
        name="qkv_prep",
    )(z, z, z, cs, q_norm_w, kv_norm_w, w_q, w_ukt)


def _attn_prompt_kernel(ql_ref, qr_ref, kc_ref, kr_ref, mc_ref, mr_ref, wuv_ref, o_ref,
                        m_sc, l_sc, acc_sc, *, tq, tk):
    qi = pl.program_id(1)
    kj = pl.program_id(2)
    rows = N_HEADS * tq
    ql = ql_ref[...].reshape(rows, KV_LORA)
    qr = qr_ref[...].reshape(rows, ROPE)

    @pl.when(kj == 0)
    def _():
        mc = mc_ref[...]
        s = _nt(ql, mc) + _nt(qr, mr_ref[...])
        col = lax.broadcasted_iota(jnp.int32, s.shape, 1)
        s = jnp.where(col < N_META, s, NEG)
        m = jnp.max(s, axis=-1, keepdims=True)
        p = jnp.exp(s - m)
        m_sc[...] = m
        l_sc[...] = jnp.sum(p, axis=-1, keepdims=True)
        acc_sc[...] = _nn(p.astype(BF16), mc)

    @pl.when(kj * tk <= qi * tq + (tq - 1))
    def _():
        kc = kc_ref[...]
        s = _nt(ql, kc) + _nt(qr, kr_ref[...])
        row = lax.broadcasted_iota(jnp.int32, s.shape, 0)
        col = lax.broadcasted_iota(jnp.int32, s.shape, 1)
        s = jnp.where(kj * tk + col <= qi * tq + (row & (tq - 1)), s, NEG)
        m_prev = m_sc[...]
        m_new = jnp.maximum(m_prev, jnp.max(s, axis=-1, keepdims=True))
        alpha = jnp.exp(m_prev - m_new)
        p = jnp.exp(s - m_new)
        l_sc[...] = alpha * l_sc[...] + jnp.sum(p, axis=-1, keepdims=True)
        acc_sc[...] = alpha * acc_sc[...] + _nn(p.astype(BF16), kc)
        m_sc[...] = m_new

    @pl.when(kj == pl.num_programs(2) - 1)
    def _():
        o = (acc_sc[...] / l_sc[...]).astype(BF16)
        for h in range(N_HEADS):
            o_ref[:, h * HEAD_DIM:(h + 1) * HEAD_DIM] = _nn(o[h * tq:(h + 1) * tq], wuv_ref[h])


def _attn_prompt(ql, qr, ckv_b, kr_b, meta_c, meta_r, w_uvt, n_batch):
    n = ckv_b.shape[0]
    seq = n // n_batch
    tq = 128
    tk = min(512, seq)
    assert seq % tq == 0 and seq % tk == 0 and tq & (tq - 1) == 0
    qb, kb = seq // tq, seq // tk
    rows = N_HEADS * tq

    def kmap(b, i, j):
        return (b * kb + jnp.minimum(j, (i * tq + tq - 1) // tk), 0)

    return pl.pallas_call(
        functools.partial(_attn_prompt_kernel, tq=tq, tk=tk),
        grid=(n_batch, qb, kb),
        in_specs=[
            pl.BlockSpec((N_HEADS, tq, KV_LORA), lambda b, i, j: (0, b * qb + i, 0)),
            pl.BlockSpec((N_HEADS, tq, ROPE), lambda b, i, j: (0, b * qb + i, 0)),
            pl.BlockSpec((tk, KV_LORA), kmap),
            pl.BlockSpec((tk, ROPE), kmap),
            pl.BlockSpec((LANES, KV_LORA), lambda b, i, j: (0, 0)),
            pl.BlockSpec((LANES, ROPE), lambda b, i, j: (0, 0)),
            pl.BlockSpec((N_HEADS, KV_LORA, HEAD_DIM), lambda b, i, j: (0, 0, 0)),
        ],
        out_specs=pl.BlockSpec((tq, D_HALF), lambda b, i, j: (b * qb + i, 0)),
        out_shape=jax.ShapeDtypeStruct((n, D_HALF), F32),
        scratch_shapes=[pltpu.VMEM((rows, 1), F32), pltpu.VMEM((rows, 1), F32),
                        pltpu.VMEM((rows, KV_LORA), F32)],
        compiler_params=_params("parallel", "parallel", "arbitrary"),
        name="mla_prompt",
    )(ql, qr, ckv_b, kr_b, meta_c, meta_r, w_uvt)


def _attn_decode_kernel(pt_ref, ql_ref, qr_ref, nc_ref, nr_ref, *rest, n_pages, n_new):
    page_c = rest[:n_pages]
    page_r = rest[n_pages:2 * n_pages]
    wuv_ref, o_ref, m_sc, l_sc, acc_sc = rest[2 * n_pages:]
    j = pl.program_id(1)
    rows = N_HEADS * n_new
    ql = ql_ref[...]
    qr = qr_ref[...]

    @pl.when(j == 0)
    def _():
        qlf = ql.astype(F32)
        qrf = qr.astype(F32)
        nc = nc_ref[...]
        nr = nr_ref[...]
        tok = lax.rem(lax.broadcasted_iota(jnp.int32, (rows, 1), 0), n_new)
        cols = []
        for t in range(n_new):
            sc = (jnp.sum(qlf * nc[t:t + 1, :], axis=-1, keepdims=True)
                  + jnp.sum(qrf * nr[t:t + 1, :], axis=-1, keepdims=True))
            cols.append(jnp.where(tok >= t, sc, NEG))
        m = functools.reduce(jnp.maximum, cols)
        ps = [jnp.exp(c - m) for c in cols]
        m_sc[...] = m
        l_sc[...] = functools.reduce(jnp.add, ps)
        acc_sc[...] = functools.reduce(jnp.add, [p * nc[t:t + 1, :] for t, p in enumerate(ps)])

    kc = jnp.concatenate([r[...].astype(BF16) for r in page_c], axis=0)
    kr = jnp.concatenate([r[...].astype(BF16) for r in page_r], axis=0)
    s = _nt(ql, kc) + _nt(qr, kr)
    m_prev = m_sc[...]
    m_new = jnp.maximum(m_prev, jnp.max(s, axis=-1, keepdims=True))
    alpha = jnp.exp(m_prev - m_new)
    p = jnp.exp(s - m_new)
    l_sc[...] = alpha * l_sc[...] + jnp.sum(p, axis=-1, keepdims=True)
    acc_sc[...] = alpha * acc_sc[...] + _nn(p.astype(BF16), kc)
    m_sc[...] = m_new

    @pl.when(j == pl.num_programs(1) - 1)
    def _():
        o = (acc_sc[...] / l_sc[...]).astype(BF16)
        head = lax.broadcasted_iota(jnp.int32, (rows, 1), 0) // n_new
        y = jnp.zeros((rows, HEAD_DIM), F32)
        for h in range(N_HEADS):
            y = jnp.where(head == h, _nn(o, wuv_ref[h]), y)
        o_ref[...] = y


def _attn_decode(ql, qr, new_c, new_r, cache_c, cache_r, page_table, w_uvt):
    nb, rows, _ = ql.shape
    n_new = rows // N_HEADS
    total_pages = page_table.shape[1]
    n_pages = min(PAGES_PER_STEP, total_pages)
    assert total_pages % n_pages == 0
    page = cache_c.shape[1]

    def pmap(k):
        return lambda b, j, pt: (pt[b, j * n_pages + k], 0, 0)

    row3 = lambda b, j, pt: (b, 0, 0)
    grid_spec = pltpu.PrefetchScalarGridSpec(
        num_scalar_prefetch=1,
        grid=(nb, total_pages // n_pages),
        in_specs=[
            pl.BlockSpec((None, rows, KV_LORA), row3),
            pl.BlockSpec((None, rows, ROPE), row3),
            pl.BlockSpec((None, n_new, KV_LORA), row3),
            pl.BlockSpec((None, n_new, ROPE), row3),
            *[pl.BlockSpec((None, page, KV_LORA), pmap(k)) for k in range(n_pages)],
            *[pl.BlockSpec((None, page, ROPE), pmap(k)) for k in range(n_pages)],
            pl.BlockSpec((N_HEADS, KV_LORA, HEAD_DIM), lambda b, j, pt: (0, 0, 0)),
        ],
        out_specs=pl.BlockSpec((None, rows, HEAD_DIM), row3),
        scratch_shapes=[pltpu.VMEM((rows, 1), F32), pltpu.VMEM((rows, 1), F32),
                        pltpu.VMEM((rows, KV_LORA), F32)],
    )
    return pl.pallas_call(
        functools.partial(_attn_decode_kernel, n_pages=n_pages, n_new=n_new),
        grid_spec=grid_spec,
        out_shape=jax.ShapeDtypeStruct((nb, rows, HEAD_DIM), F32),
        compiler_params=_params("parallel", "arbitrary"),
        name="mla_decode",
    )(page_table, ql, qr, new_c, new_r, *([cache_c] * n_pages), *([cache_r] * n_pages), w_uvt)


def _hgrn_kernel(qb_ref, fb_ref, ib_ref, gb_ref, lb_ref, nw_ref, s0_ref, y_ref, s_ref,
                 *, nbt, chunk, l_valid, l_total):
    c = pl.program_id(1)

    @pl.when(c == 0)
    def _():
        s_ref[...] = s0_ref[...]

    lb = lb_ref[...]
    nw = nw_ref[...]
    t_row = lax.broadcasted_iota(jnp.int32, (chunk, chunk), 0)
    t_col = lax.broadcasted_iota(jnp.int32, (chunk, chunk), 1)
    causal = t_row >= t_col
    tri = jnp.where(causal, 1.0, 0.0).astype(BF16)
    ones = jnp.ones((chunk, HEAD_DIM), BF16)
    tok = lax.broadcasted_iota(jnp.int32, (chunk, 1), 0)

    def colsum(x):
        hi, mid, lo = _split3(x)
        return _tn(hi, ones) + _tn(mid, ones) + _tn(lo, ones)

    def batch_body(nb, carry):
        fb = fb_ref[nb]
        logf = jnp.log(lb + (1.0 - lb) * jax.nn.sigmoid(fb))
        kk = (1.0 - lb) * jax.nn.sigmoid(-fb)
        if l_valid < l_total:
            valid = c * chunk + tok < l_valid
            logf = jnp.where(valid, logf, 0.0)
            kk = jnp.where(valid, kk, 0.0)
        q = _silu(qb_ref[nb])
        v = ib_ref[nb]
        gate = _silu(gb_ref[nb])
        safe = jnp.max(-jnp.sum(logf, axis=0, keepdims=True)) < HGRN_SAFE_DECAY

        def readout(h, o):
            hs = slice(h * HEAD_DIM, (h + 1) * HEAD_DIM)
            y_ref[nb, :, hs] = (_rms(o) * nw[:, hs] * gate[:, hs]).astype(BF16)

        @pl.when(safe)
        def _():
            for h in range(N_HEADS):
                hs = slice(h * HEAD_DIM, (h + 1) * HEAD_DIM)
                lf = logf[:, hs]
                hi, mid, lo = _split3(lf)
                b = _nn(tri, hi) + _nn(tri, mid) + _nn(tri, lo)
                qd = (q[:, hs] * jnp.exp(b)).astype(BF16)
                ke = (kk[:, hs] * jnp.exp(-b)).astype(BF16)
                vh = v[:, hs].astype(BF16)
                a = jnp.where(causal, _nt(qd, ke), 0.0).astype(BF16)
                s_old = s_ref[nb, h]
                readout(h, _nn(a, vh) + _nn(qd, s_old.astype(BF16)))
                wd = (kk[:, hs] * jnp.exp(b[chunk - 1:chunk, :] - b)).astype(BF16)
                s_ref[nb, h] = jnp.exp(colsum(lf)) * s_old + _tn(wd, vh)

        @pl.when(jnp.logical_not(safe))
        def _():
            for h in range(N_HEADS):
                hs = slice(h * HEAD_DIM, (h + 1) * HEAD_DIM)
                lf, kh, qh = logf[:, hs], kk[:, hs], q[:, hs]
                vh = v[:, hs].astype(BF16)

                def token(t, st):
                    s_cur, o = st
                    here = tok == t
                    s_cur = (jnp.exp(colsum(jnp.where(here, lf, 0.0))) * s_cur
                             + _tn(jnp.where(here, kh, 0.0).astype(BF16), vh))
                    o = o + _nn(jnp.where(here, qh, 0.0).astype(BF16), s_cur.astype(BF16))
                    return s_cur, o

                s_new, o = lax.fori_loop(0, chunk, token,
                                         (s_ref[nb, h], jnp.zeros((chunk, HEAD_DIM), F32)))
                readout(h, o)
                s_ref[nb, h] = s_new

        return carry

    lax.fori_loop(0, nbt, batch_body, 0)


def _hgrn(z3, lb, norm_w, s0, *, chunk, nbt, l_valid):
    nb, l_total, _ = z3.shape
    assert l_total % chunk == 0 and nb % nbt == 0

    def zspec(k):
        return pl.BlockSpec((nbt, chunk, D_HALF), lambda b, c: (b, c, k))

    sspec = pl.BlockSpec((nbt, N_HEADS, HEAD_DIM, HEAD_DIM), lambda b, c: (b, 0, 0, 0))
    vec = pl.BlockSpec((1, D_HALF), lambda b, c: (0, 0))
    return pl.pallas_call(
        functools.partial(_hgrn_kernel, nbt=nbt, chunk=chunk, l_valid=l_valid, l_total=l_total),
        grid=(nb // nbt, l_total // chunk),
        in_specs=[zspec(1), zspec(2), zspec(3), zspec(4), vec, vec, sspec],
        out_specs=[pl.BlockSpec((nbt, chunk, D_HALF), lambda b, c: (b, c, 0)), sspec],
        out_shape=[jax.ShapeDtypeStruct((nb, l_total, D_HALF), BF16),
                   jax.ShapeDtypeStruct(s0.shape, F32)],
        compiler_params=_params("parallel", "arbitrary"),
        name="hgrn",
    )(z3, z3, z3, z3, lb, norm_w, s0)


def _outproj_router_kernel(hp_ref, hs_ref, yap_ref, yas_ref, ybp_ref, ybs_ref, anw_ref, wout_ref,
                           fnw_ref, wrh_ref, wrl_ref, br_ref,
                           h2_ref, xf_ref, slab_ref, cnt_ref, *, tiles_p, tm):
    i = pl.program_id(0)
    is_p = i < tiles_p
    h = jnp.where(is_p, hp_ref[...], hs_ref[...])
    ya = jnp.where(is_p, yap_ref[...], yas_ref[...])
    yb = jnp.where(is_p, ybp_ref[...], ybs_ref[...])
    ya = (_rms(ya) * anw_ref[...]).astype(BF16)
    h2 = h + (_nn(ya, wout_ref[:D_HALF, :]) + _nn(yb, wout_ref[D_HALF:, :]))
    h2_ref[...] = h2
    xf = _rms(h2) * fnw_ref[...]
    x_hi = xf.astype(BF16)
    xf_ref[...] = x_hi
    x_lo = (xf - x_hi.astype(F32)).astype(BF16)
    w_hi = wrh_ref[...]
    logits = _nn(x_hi, w_hi) + _nn(x_lo, w_hi) + _nn(x_hi, wrl_ref[...]) + br_ref[...]
    lane = lax.broadcasted_iota(jnp.int32, logits.shape, 1)

    def first_max(x):
        m = jnp.max(x, axis=-1, keepdims=True)
        return m, jnp.min(jnp.where(x == m, lane, LANES), axis=-1, keepdims=True)

    gl = jnp.where(lane < N_GROUPS, logits, NEG)
    g_max, g_idx = first_max(gl)
    pg_top = 1.0 / jnp.sum(jnp.exp(gl - g_max), axis=-1, keepdims=True)
    lo = N_GROUPS + EXP_PER_GROUP * g_idx
    el = jnp.where((lane >= lo) & (lane < lo + EXP_PER_GROUP), logits, NEG)
    e1, i1 = first_max(el)
    e2, i2 = first_max(jnp.where(lane == i1, NEG, el))
    r = jnp.exp(e2 - e1)
    w1 = pg_top / (1.0 + r)
    w2 = w1 * r

    @pl.when(i == 0)
    def _():
        cnt_ref[...] = jnp.zeros_like(cnt_ref)

    onehot = jnp.where((lane == i1) | (lane == i2), 1.0, 0.0)
    t_row = lax.broadcasted_iota(jnp.int32, (tm, tm), 0)
    t_col = lax.broadcasted_iota(jnp.int32, (tm, tm), 1)
    before = jnp.where(t_row > t_col, 1.0, 0.0).astype(BF16)
    prefix = _nn(before, onehot.astype(BF16)) + cnt_ref[0:1, :]
    r1 = jnp.sum(jnp.where(lane == i1, prefix, 0.0), axis=-1, keepdims=True)
    r2 = jnp.sum(jnp.where(lane == i2, prefix, 0.0), axis=-1, keepdims=True)
    cnt_ref[...] = cnt_ref[...] + jnp.sum(onehot, axis=0, keepdims=True)
    cols = [(i1 - N_GROUPS).astype(F32), (i2 - N_GROUPS).astype(F32), w1, w2, r1, r2]
    slab = jnp.zeros(logits.shape, F32)
    for k, val in enumerate(cols):
        slab = jnp.where(lane == k, val, slab)
    slab_ref[...] = slab


def _outproj_router(h_p, h_s, ya_p, ya_s, yb_p, yb_s, a_norm_w, w_out, f_norm_w, wr_hi, wr_lo, b_r):
    n_p, n_s = h_p.shape[0], h_s.shape[0]
    tm = _pick_tile(n_s, 256)
    assert n_p % tm == 0
    tiles_p, tiles_s = n_p // tm, n_s // tm
    n = n_p + n_s
    pmap = lambda i: (jnp.minimum(i, tiles_p - 1), 0)
    smap = lambda i: (jnp.maximum(i - tiles_p, 0), 0)
    const = lambda i: (0, 0)
    row = lambda i: (i, 0)
    return pl.pallas_call(
        functools.partial(_outproj_router_kernel, tiles_p=tiles_p, tm=tm),
        grid=(tiles_p + tiles_s,),
        in_specs=[
            pl.BlockSpec((tm, D_MODEL), pmap), pl.BlockSpec((tm, D_MODEL), smap),
            pl.BlockSpec((tm, D_HALF), pmap), pl.BlockSpec((tm, D_HALF), smap),
            pl.BlockSpec((tm, D_HALF), pmap), pl.BlockSpec((tm, D_HALF), smap),
            pl.BlockSpec((1, D_HALF), const),
            pl.BlockSpec((D_MODEL, D_MODEL), const),
            pl.BlockSpec((1, D_MODEL), const),
            pl.BlockSpec((D_MODEL, LANES), const), pl.BlockSpec((D_MODEL, LANES), const),
            pl.BlockSpec((1, LANES), const),
        ],
        out_specs=[pl.BlockSpec((tm, D_MODEL), row), pl.BlockSpec((tm, D_MODEL), row),
                   pl.BlockSpec((tm, LANES), row), pl.BlockSpec((8, LANES), const)],
        out_shape=[jax.ShapeDtypeStruct((n, D_MODEL), F32), jax.ShapeDtypeStruct((n, D_MODEL), BF16),
                   jax.ShapeDtypeStruct((n, LANES), F32), jax.ShapeDtypeStruct((8, LANES), F32)],
        compiler_params=_params("arbitrary"),
        name="outproj_router",
    )(h_p, h_s, ya_p, ya_s, yb_p, yb_s, a_norm_w, w_out, f_norm_w, wr_hi, wr_lo, b_r)


def _expert_kernel(te_ref, tv_ref, x_ref, wg_ref, wu_ref, wd_ref, y_ref):
    t = pl.program_id(0)

    @pl.when(tv_ref[t] == 1)
    def _():
        x = x_ref[...]
        hdn = (_silu(_nn(x, wg_ref[...])) * _nn(x, wu_ref[...])).astype(BF16)
        y_ref[...] = _nn(hdn, wd_ref[...])

    @pl.when(tv_ref[t] == 0)
    def _():
        y_ref[...] = jnp.zeros_like(y_ref)


def _expert_ffn(tile_expert, tile_valid, x_sorted, w_gate, w_up, w_down):
    n = x_sorted.shape[0]
    grid_spec = pltpu.PrefetchScalarGridSpec(
        num_scalar_prefetch=2,
        grid=(n // MOE_TILE,),
        in_specs=[
            pl.BlockSpec((MOE_TILE, D_MODEL), lambda t, te, tv: (t, 0)),
            pl.BlockSpec((None, D_MODEL, D_FF), lambda t, te, tv: (te[t], 0, 0)),
            pl.BlockSpec((None, D_MODEL, D_FF), lambda t, te, tv: (te[t], 0, 0)),
            pl.BlockSpec((None, D_FF, D_MODEL), lambda t, te, tv: (te[t], 0, 0)),
        ],
        out_specs=pl.BlockSpec((MOE_TILE, D_MODEL), lambda t, te, tv: (t, 0)),
    )
    return pl.pallas_call(
        _expert_kernel,
        grid_spec=grid_spec,
        out_shape=jax.ShapeDtypeStruct((n, D_MODEL), F32),
        compiler_params=_params("arbitrary"),
        name="expert_ffn",
    )(tile_expert, tile_valid, x_sorted, w_gate, w_up, w_down)


def _final_kernel(h2_ref, y0_ref, y1_ref, slab_ref, nw_ref, op_ref, os_ref, *, tiles_p):
    i = pl.program_id(0)
    slab = slab_ref[...]
    h = h2_ref[...] + (slab[:, 2:3] * y0_ref[...] + slab[:, 3:4] * y1_ref[...])
    y = _rms(h) * nw_ref[...]

    @pl.when(i < tiles_p)
    def _():
        op_ref[...] = y

    @pl.when(i >= tiles_p)
    def _():
        os_ref[...] = y


def _final(h2, y_pair, slab, norm_w, n_p):
    n = h2.shape[0]
    n_s = n - n_p
    tm = _pick_tile(n_s, 256)
    tiles_p, tiles_s = n_p // tm, n_s // tm
    return pl.pallas_call(
        functools.partial(_final_kernel, tiles_p=tiles_p),
        grid=(tiles_p + tiles_s,),
        in_specs=[
            pl.BlockSpec((tm, D_MODEL), lambda i: (i, 0)),
            pl.BlockSpec((tm, D_MODEL), lambda i: (i, 0)),
            pl.BlockSpec((tm, D_MODEL), lambda i: (i, 1)),
            pl.BlockSpec((tm, LANES), lambda i: (i, 0)),
            pl.BlockSpec((1, D_MODEL), lambda i: (0, 0)),
        ],
        out_specs=[pl.BlockSpec((tm, D_MODEL), lambda i: (jnp.minimum(i, tiles_p - 1), 0)),
                   pl.BlockSpec((tm, D_MODEL), lambda i: (jnp.maximum(i - tiles_p, 0), 0))],
        out_shape=[jax.ShapeDtypeStruct((n_p, D_MODEL), F32), jax.ShapeDtypeStruct((n_s, D_MODEL), F32)],
        compiler_params=_params("arbitrary"),
        name="combine_final_norm",
    )(h2, y_pair, y_pair, slab, norm_w)


def _rope_table(pos):
    half = ROPE // 2
    inv_freq = jnp.power(ROPE_THETA, -jnp.arange(half, dtype=F32) / half)
    ang = pos.astype(F32)[:, None] * inv_freq[None, :]
    cos, sin = jnp.cos(ang), jnp.sin(ang)
    return jnp.concatenate([cos, cos, sin, sin], axis=-1)


def _rotate_half_cols(w):
    half = ROPE // 2
    return jnp.concatenate([-w[..., half:], w[..., :half]], axis=-1)


def kernel(x_prompt, x_sample, cache_ckv, cache_krope, state_hgrn, page_table, meta_tokens, attn_norm_w, w_in, q_norm_w, w_uq, kv_norm_w, w_uk, w_uv, mla_out_norm_w, hgrn_lb_logits, hgrn_out_norm_w, w_out, ffn_norm_w, w_router_group, b_router_group, w_router_expert, b_router_expert, w_gate, w_up, w_down, final_norm_w):
    n_batch, seq, _ = x_prompt.shape
    n_dec, n_new, _ = x_sample.shape
    depth = w_in.shape[0]
    assert depth == 1
    n_p, n_s = n_batch * seq, n_dec * n_new
    past = page_table.shape[1] * cache_ckv.shape[2]

    wi = w_in[0]
    c0 = Q_LORA + KV_LORA
    w_kr = wi[:, c0:c0 + ROPE]
    w_z = jnp.concatenate([wi[:, :c0], wi[:, c0 + ROPE:], w_kr, _rotate_half_cols(w_kr),
                           jnp.zeros((D_MODEL, LANES), F32)], axis=1).astype(BF16)
    uq = w_uq[0]
    uq_rope = uq[:, :, HEAD_DIM:]
    w_q = jnp.concatenate([uq[:, :, :HEAD_DIM].reshape(Q_LORA, D_HALF),
                           uq_rope.reshape(Q_LORA, N_HEADS * ROPE),
                           _rotate_half_cols(uq_rope).reshape(Q_LORA, N_HEADS * ROPE)], axis=1).astype(BF16)
    w_ukt = jnp.transpose(w_uk[0], (1, 2, 0)).astype(BF16)
    w_uvt = jnp.transpose(w_uv[0], (1, 0, 2)).astype(BF16)
    w_o = w_out[0].astype(BF16)
    w_r = jnp.concatenate([w_router_group[0], w_router_expert[0],
                           jnp.zeros((D_MODEL, LANES - N_GROUPS - N_EXPERTS), F32)], axis=1)
    wr_hi = w_r.astype(BF16)
    wr_lo = (w_r - wr_hi.astype(F32)).astype(BF16)
    b_r = jnp.concatenate([b_router_group[0], b_router_expert[0],
                           jnp.zeros((LANES - N_GROUPS - N_EXPERTS,), F32)])[None, :]
    wg, wu, wd = w_gate[0].astype(BF16), w_up[0].astype(BF16), w_down[0].astype(BF16)
    lb = jax.nn.softmax(hgrn_lb_logits.astype(F32), axis=0)[0][None, :]
    a_nw, q_nw, kv_nw = attn_norm_w[0][None, :], q_norm_w[0][None, :], kv_norm_w[0][None, :]
    mo_nw, ho_nw, f_nw = mla_out_norm_w[0][None, :], hgrn_out_norm_w[0][None, :], ffn_norm_w[0][None, :]

    z_m = _in_proj(meta_tokens, a_nw, w_z)
    _, _, ckv_m, kr_m, ckvb_m, krb_m = _qkv_prep(z_m, _rope_table(jnp.arange(N_META)), q_nw, kv_nw, w_q, w_ukt)
    _, s_meta = _hgrn(z_m[None], lb, ho_nw, jnp.zeros((1, N_HEADS, HEAD_DIM, HEAD_DIM), F32),
                      chunk=N_META, nbt=1, l_valid=N_META)
    meta_c = jnp.pad(ckvb_m, ((0, LANES - N_META), (0, 0)))
    meta_r = jnp.pad(krb_m, ((0, LANES - N_META), (0, 0)))

    xp = x_prompt.reshape(n_p, D_MODEL)
    z_p = _in_proj(xp, a_nw, w_z)
    cs_p = jnp.tile(_rope_table(N_META + jnp.arange(seq)), (n_batch, 1))
    ql_p, qr_p, ckv_p, kr_p, ckvb_p, krb_p = _qkv_prep(z_p, cs_p, q_nw, kv_nw, w_q, w_ukt)
    ya_p = _attn_prompt(ql_p, qr_p, ckvb_p, krb_p, meta_c, meta_r, w_uvt, n_batch)
    s0_p = jnp.broadcast_to(s_meta, (n_batch, N_HEADS, HEAD_DIM, HEAD_DIM))
    yb_p, st_p = _hgrn(z_p.reshape(n_batch, seq, Z_WIDTH), lb, ho_nw, s0_p,
                       chunk=min(64, seq), nbt=n_batch, l_valid=seq)

    xs = x_sample.reshape(n_s, D_MODEL)
    z_s = _in_proj(xs, a_nw, w_z)
    cs_s = jnp.tile(_rope_table(past + jnp.arange(n_new)), (n_dec, 1))
    ql_s, qr_s, ckv_s, kr_s, _, _ = _qkv_prep(z_s, cs_s, q_nw, kv_nw, w_q, w_ukt)

    def dec_rows(a):
        d = a.shape[-1]
        return jnp.transpose(a.reshape(N_HEADS, n_dec, n_new, d), (1, 0, 2, 3)).reshape(n_dec, N_HEADS * n_new, d)

    ya_s = _attn_decode(dec_rows(ql_s), dec_rows(qr_s), ckv_s.reshape(n_dec, n_new, KV_LORA),
                        kr_s.reshape(n_dec, n_new, ROPE), cache_ckv[0], cache_krope[0], page_table, w_uvt)
    ya_s = jnp.transpose(ya_s.reshape(n_dec, N_HEADS, n_new, HEAD_DIM), (0, 2, 1, 3)).reshape(n_s, D_HALF)
    pad_new = -n_new % 8
    z_s3 = jnp.pad(z_s.reshape(n_dec, n_new, Z_WIDTH), ((0, 0), (0, pad_new), (0, 0)))
    yb_s, st_s = _hgrn(z_s3, lb, ho_nw, state_hgrn[0].astype(F32), chunk=n_new + pad_new,
                       nbt=_pick_tile(n_dec, 8), l_valid=n_new)
    yb_s = yb_s[:, :n_new].reshape(n_s, D_HALF)

    h2, xf, slab, cnt = _outproj_router(xp, xs, ya_p, ya_s, yb_p.reshape(n_p, D_HALF), yb_s,
                                        mo_nw, w_o, f_nw, wr_hi, wr_lo, b_r)

    n = n_p + n_s
    e_idx = slab[:, 0:2].astype(jnp.int32)
    rank = slab[:, 4:6].astype(jnp.int32)
    counts = cnt[0, N_GROUPS:N_GROUPS + N_EXPERTS].astype(jnp.int32)
    padded = (counts + MOE_TILE - 1) // MOE_TILE * MOE_TILE
    ends = jnp.cumsum(padded)
    pos = (ends - padded)[e_idx] + rank
    n_tiles = -(-2 * n // MOE_TILE) + N_EXPERTS
    starts = jnp.arange(n_tiles, dtype=jnp.int32) * MOE_TILE
    tile_valid = (starts < ends[-1]).astype(jnp.int32)
    tile_expert = jnp.searchsorted(ends, starts, side="right").astype(jnp.int32)
    last_valid = jnp.maximum(ends[-1] // MOE_TILE - 1, 0)
    tile_expert = tile_expert[jnp.minimum(jnp.arange(n_tiles), last_valid)]
    token_of_row = jnp.zeros((n_tiles * MOE_TILE,), jnp.int32).at[pos.reshape(-1)].set(
        jnp.repeat(jnp.arange(n, dtype=jnp.int32), 2))
    x_sorted = jnp.take(xf, token_of_row, axis=0)
    y_sorted = _expert_ffn(tile_expert, tile_valid, x_sorted, wg, wu, wd)
    y_pair = jnp.take(y_sorted, pos.reshape(-1), axis=0).reshape(n, 2 * D_MODEL)

    y_p, y_s = _final(h2, y_pair, slab, final_norm_w[None, :], n_p)

    def with_meta(m, p, d):
        return jnp.concatenate([jnp.broadcast_to(m[None], (n_batch, N_META, d)), p.reshape(n_batch, seq, d)], axis=1)[None]

    return (y_p.reshape(n_batch, seq, D_MODEL),
            y_s.reshape(n_dec, n_new, D_MODEL),
            with_meta(ckv_m, ckv_p, KV_LORA),
            with_meta(kr_m, kr_p, ROPE),
            st_p[None].astype(x_prompt.dtype),
            ckv_s.reshape(n_dec, n_new, KV_LORA)[None],
            kr_s.reshape(n_dec, n_new, ROPE)[None],
            st_s[None].astype(state_hgrn.dtype))
```

```python
import functools
import math

import jax
import jax.numpy as jnp
from jax import lax
from jax.experimental import pallas as pl
from jax.experimental.pallas import tpu as pltpu

F32 = jnp.float32
BF16 = jnp.bfloat16

D_MODEL = 2048
N_META = 16
N_HEADS = 8
HEAD_DIM = 128
ROPE = 64
Q_LORA = 512
KV_LORA = 512
D_HALF = N_HEADS * HEAD_DIM
ROPE_THETA = 10000.0
SM_SCALE = (HEAD_DIM + ROPE) ** -0.5
EPS = 1e-6
N_GROUPS = 4
EXP_PER_GROUP = 8
N_EXPERTS = N_GROUPS * EXP_PER_GROUP
D_FF = 512

Z_WIDTH = 5376
Z_TILE = 1792
Z_ROPE_BLOCK = 40

LANES = 128
VMEM_LIMIT = 56 * 1024 * 1024
NEG = -1e30
HGRN_SAFE_DECAY = 60.0
MOE_TILE = 256
PAGES_PER_STEP = 32
PAGES_PER_GROUP = 4


def _nn(a, b):
    return jnp.dot(a, b, preferred_element_type=F32)


def _nt(a, b):
    return lax.dot_general(a, b, (((1,), (1,)), ((), ())), preferred_element_type=F32)


def _tn(a, b):
    return lax.dot_general(a, b, (((0,), (0,)), ((), ())), preferred_element_type=F32)


def _split3(x):
    hi = x.astype(BF16)
    r1 = x - hi.astype(F32)
    mid = r1.astype(BF16)
    lo = (r1 - mid.astype(F32)).astype(BF16)
    return hi, mid, lo


def _rms(x):
    return x * lax.rsqrt(jnp.mean(x * x, axis=-1, keepdims=True) + EPS)


def _silu(x):
    return x * jax.nn.sigmoid(x)


def _pick_tile(n, pref):
    if n <= pref:
        return n
    for t in range(pref, 7, -1):
        if n % t == 0 and t % 8 == 0:
            return t
    raise ValueError(f"no tile for {n}")


def _params(*sem):
    return pltpu.CompilerParams(dimension_semantics=sem, vmem_limit_bytes=VMEM_LIMIT)


def _inproj_kernel(x_ref, nw_ref, w_ref, z_ref, xn_ref):
    @pl.when(pl.program_id(1) == 0)
    def _():
        xn_ref[...] = (_rms(x_ref[...]) * nw_ref[...]).astype(BF16)

    z_ref[...] = _nn(xn_ref[...], w_ref[...])


def _in_proj(x, norm_w, w_z):
    n = x.shape[0]
    tm = _pick_tile(n, 512)
    return pl.pallas_call(
        _inproj_kernel,
        grid=(n // tm, Z_WIDTH // Z_TILE),
        in_specs=[
            pl.BlockSpec((tm, D_MODEL), lambda i, j: (i, 0)),
            pl.BlockSpec((1, D_MODEL), lambda i, j: (0, 0)),
            pl.BlockSpec((D_MODEL, Z_TILE), lambda i, j: (0, j)),
        ],
        out_specs=pl.BlockSpec((tm, Z_TILE), lambda i, j: (i, j)),
        out_shape=jax.ShapeDtypeStruct((n, Z_WIDTH), F32),
        scratch_shapes=[pltpu.VMEM((tm, D_MODEL), BF16)],
        compiler_params=_params("parallel", "arbitrary"),
        name="in_proj",
    )(x, norm_w, w_z)


def _prep_kernel(zq_ref, zkv_ref, zkr_ref, cs_ref, qnw_ref, kvnw_ref, wq_ref, wuk_ref,
                 ql_ref, qr_ref, ckv_ref, kr_ref, ckvb_ref, krb_ref):
    cqn = (_rms(zq_ref[...]) * qnw_ref[...]).astype(BF16)
    q = _nn(cqn, wq_ref[...])
    cs = cs_ref[...]
    cos = cs[:, :ROPE]
    sin = cs[:, ROPE:]
    for h in range(N_HEADS):
        qn = q[:, h * HEAD_DIM:(h + 1) * HEAD_DIM].astype(BF16)
        ql_ref[h] = (_nn(qn, wuk_ref[h]) * SM_SCALE).astype(BF16)
        a = q[:, D_HALF + h * ROPE:D_HALF + (h + 1) * ROPE]
        b = q[:, D_HALF + N_HEADS * ROPE + h * ROPE:D_HALF + N_HEADS * ROPE + (h + 1) * ROPE]
        qr_ref[h] = ((a * cos + b * sin) * SM_SCALE).astype(BF16)
    ckv = _rms(zkv_ref[...]) * kvnw_ref[...]
    ckv_ref[...] = ckv
    ckvb_ref[...] = ckv.astype(BF16)
    prod = zkr_ref[...] * cs
    kr = prod[:, :ROPE] + prod[:, ROPE:]
    kr_ref[...] = kr
    krb_ref[...] = kr.astype(BF16)


def _qkv_prep(z, cs, q_norm_w, kv_norm_w, w_q, w_ukt):
    n = z.shape[0]
    tm = _pick_tile(n, 256)
    const2 = lambda i: (0, 0)
    return pl.pallas_call(
        _prep_kernel,
        grid=(n // tm,),
        in_specs=[
            pl.BlockSpec((tm, Q_LORA), lambda i: (i, 0)),
            pl.BlockSpec((tm, KV_LORA), lambda i: (i, 1)),
            pl.BlockSpec((tm, LANES), lambda i: (i, Z_ROPE_BLOCK)),
            pl.BlockSpec((tm, LANES), lambda i: (i, 0)),
            pl.BlockSpec((1, Q_LORA), const2),
            pl.BlockSpec((1, KV_LORA), const2),
            pl.BlockSpec((Q_LORA, 2 * D_HALF), const2),
            pl.BlockSpec((N_HEADS, HEAD_DIM, KV_LORA), lambda i: (0, 0, 0)),
        ],
        out_specs=[
            pl.BlockSpec((N_HEADS, tm, KV_LORA), lambda i: (0, i, 0)),
            pl.BlockSpec((N_HEADS, tm, ROPE), lambda i: (0, i, 0)),
            pl.BlockSpec((tm, KV_LORA), lambda i: (i, 0)),
            pl.BlockSpec((tm, ROPE), lambda i: (i, 0)),
            pl.BlockSpec((tm, KV_LORA), lambda i: (i, 0)),
            pl.BlockSpec((tm, ROPE), lambda i: (i, 0)),
        ],
        out_shape=[
            jax.ShapeDtypeStruct((N_HEADS, n, KV_LORA), BF16),
            jax.ShapeDtypeStruct((N_HEADS, n, ROPE), BF16),
            jax.ShapeDtypeStruct((n, KV_LORA), F32),
            jax.ShapeDtypeStruct((n, ROPE), F32),
            jax.ShapeDtypeStruct((n, KV_LORA), BF16),
            jax.ShapeDtypeStruct((n, ROPE), BF16),
        ],
        compiler_params=_params("parallel"),
        name="qkv_prep",
    )(z, z, z, cs, q_norm_w, kv_norm_w, w_q, w_ukt)


def _attn_prompt_kernel(ql_ref, qr_ref, kc_ref, kr_ref, mc_ref, mr_ref, wuv_ref, o_ref,
                        m_sc, l_sc, acc_sc, *, tq, tk):
    qi = pl.program_id(1)
    kj = pl.program_id(2)
    rows = N_HEADS * tq
    ql = ql_ref[...].reshape(rows, KV_LORA)
    qr = qr_ref[...].reshape(rows, ROPE)

    @pl.when(kj == 0)
    def _():
        mc = mc_ref[...]
        s = _nt(ql, mc) + _nt(qr, mr_ref[...])
        col = lax.broadcasted_iota(jnp.int32, s.shape, 1)
        s = jnp.where(col < N_META, s, NEG)
        m = jnp.max(s, axis=-1, keepdims=True)
        p = jnp.exp(s - m)
        m_sc[...] = m
        l_sc[...] = jnp.sum(p, axis=-1, keepdims=True)
        acc_sc[...] = _nn(p.astype(BF16), mc)

    def key_block(masked):
        kc = kc_ref[...]
        s = _nt(ql, kc) + _nt(qr, kr_ref[...])
        if masked:
            row = lax.broadcasted_iota(jnp.int32, s.shape, 0)
            col = lax.broadcasted_iota(jnp.int32, s.shape, 1)
            s = jnp.where(kj * tk + col <= qi * tq + (row & (tq - 1)), s, NEG)
        m_prev = m_sc[...]
        m_new = jnp.maximum(m_prev, jnp.max(s, axis=-1, keepdims=True))
        alpha = jnp.exp(m_prev - m_new)
        p = jnp.exp(s - m_new)
        l_sc[...] = alpha * l_sc[...] + jnp.sum(p, axis=-1, keepdims=True)
        acc_sc[...] = alpha * acc_sc[...] + _nn(p.astype(BF16), kc)
        m_sc[...] = m_new

    last_key = kj * tk + (tk - 1)

    @pl.when(last_key <= qi * tq)
    def _():
        key_block(False)

    @pl.when((last_key > qi * tq) & (kj * tk <= qi * tq + (tq - 1)))
    def _():
        key_block(True)

    @pl.when(kj == pl.num_programs(2) - 1)
    def _():
        o = (acc_sc[...] / l_sc[...]).astype(BF16)
        for h in range(N_HEADS):
            o_ref[:, h * HEAD_DIM:(h + 1) * HEAD_DIM] = _nn(o[h * tq:(h + 1) * tq], wuv_ref[h])


def _attn_prompt(ql, qr, ckv_b, kr_b, meta_c, meta_r, w_uvt, n_batch):
    n = ckv_b.shape[0]
    seq = n // n_batch
    tq = 128
    tk = min(512, seq)
    assert seq % tq == 0 and seq % tk == 0 and tq & (tq - 1) == 0
    qb, kb = seq // tq, seq // tk
    rows = N_HEADS * tq

    def kmap(b, i, j):
        return (b * kb + jnp.minimum(j, (i * tq + tq - 1) // tk), 0)

    return pl.pallas_call(
        functools.partial(_attn_prompt_kernel, tq=tq, tk=tk),
        grid=(n_batch, qb, kb),
        in_specs=[
            pl.BlockSpec((N_HEADS, tq, KV_LORA), lambda b, i, j: (0, b * qb + i, 0)),
            pl.BlockSpec((N_HEADS, tq, ROPE), lambda b, i, j: (0, b * qb + i, 0)),
            pl.BlockSpec((tk, KV_LORA), kmap),
            pl.BlockSpec((tk, ROPE), kmap),
            pl.BlockSpec((LANES, KV_LORA), lambda b, i, j: (0, 0)),
            pl.BlockSpec((LANES, ROPE), lambda b, i, j: (0, 0)),
            pl.BlockSpec((N_HEADS, KV_LORA, HEAD_DIM), lambda b, i, j: (0, 0, 0)),
        ],
        out_specs=pl.BlockSpec((tq, D_HALF), lambda b, i, j: (b * qb + i, 0)),
        out_shape=jax.ShapeDtypeStruct((n, D_HALF), F32),
        scratch_shapes=[pltpu.VMEM((rows, 1), F32), pltpu.VMEM((rows, 1), F32),
                        pltpu.VMEM((rows, KV_LORA), F32)],
        compiler_params=_params("parallel", "parallel", "arbitrary"),
        name="mla_prompt",
    )(ql, qr, ckv_b, kr_b, meta_c, meta_r, w_uvt)


def _attn_decode_kernel(pt_ref, ql_ref, qr_ref, nc_ref, nr_ref, *rest, n_pages, n_new, group):
    page_c = rest[:n_pages]
    page_r = rest[n_pages:2 * n_pages]
    wuv_ref, o_ref, m_sc, l_sc, acc_sc = rest[2 * n_pages:]
    j = pl.program_id(1)
    rows = N_HEADS * n_new
    ql = ql_ref[...]
    qr = qr_ref[...]

    @pl.when(j == 0)
    def _():
        qlf = ql.astype(F32)
        qrf = qr.astype(F32)
        nc = nc_ref[...]
        nr = nr_ref[...]
        tok = lax.rem(lax.broadcasted_iota(jnp.int32, (rows, 1), 0), n_new)
        cols = []
        for t in range(n_new):
            sc = (jnp.sum(qlf * nc[t:t + 1, :], axis=-1, keepdims=True)
                  + jnp.sum(qrf * nr[t:t + 1, :], axis=-1, keepdims=True))
            cols.append(jnp.where(tok >= t, sc, NEG))
        m = functools.reduce(jnp.maximum, cols)
        ps = [jnp.exp(c - m) for c in cols]
        m_sc[...] = m
        l_sc[...] = functools.reduce(jnp.add, ps)
        acc_sc[...] = functools.reduce(jnp.add, [p * nc[t:t + 1, :] for t, p in enumerate(ps)])

    parts = []
    for g in range(0, n_pages, group):
        kc = jnp.concatenate([r[...].astype(BF16) for r in page_c[g:g + group]], axis=0)
        krt = jnp.concatenate([r[...].astype(BF16) for r in page_r[g:g + group]], axis=1)
        s = _nt(ql, kc) + _nn(qr, krt)
        m = jnp.max(s, axis=-1, keepdims=True)
        p = jnp.exp(s - m)
        parts.append((m, jnp.sum(p, axis=-1, keepdims=True), _nn(p.astype(BF16), kc)))
    m_prev = m_sc[...]
    m_new = functools.reduce(jnp.maximum, [m_prev] + [m for m, _, _ in parts])
    alpha = jnp.exp(m_prev - m_new)
    l_new = alpha * l_sc[...]
    acc = alpha * acc_sc[...]
    for m, l, a in parts:
        w = jnp.exp(m - m_new)
        l_new = l_new + w * l
        acc = acc + w * a
    l_sc[...] = l_new
    acc_sc[...] = acc
    m_sc[...] = m_new

    @pl.when(j == pl.num_programs(1) - 1)
    def _():
        o = (acc_sc[...] / l_sc[...]).astype(BF16)
        head = lax.broadcasted_iota(jnp.int32, (rows, 1), 0) // n_new
        y = jnp.zeros((rows, HEAD_DIM), F32)
        for h in range(N_HEADS):
            y = jnp.where(head == h, _nn(o, wuv_ref[h]), y)
        o_ref[...] = y


def _attn_decode(ql, qr, new_c, new_r, cache_c, cache_r, page_table, w_uvt):
    nb, rows, _ = ql.shape
    n_new = rows // N_HEADS
    total_pages = page_table.shape[1]
    n_pages = min(PAGES_PER_STEP, total_pages)
    group = min(PAGES_PER_GROUP, n_pages)
    assert total_pages % n_pages == 0 and n_pages % group == 0
    page = cache_c.shape[1]

    def pmap(k):
        return lambda b, j, pt: (pt[b, j * n_pages + k], 0, 0)

    row3 = lambda b, j, pt: (b, 0, 0)
    grid_spec = pltpu.PrefetchScalarGridSpec(
        num_scalar_prefetch=1,
        grid=(nb, total_pages // n_pages),
        in_specs=[
            pl.BlockSpec((None, rows, KV_LORA), row3),
            pl.BlockSpec((None, rows, ROPE), row3),
            pl.BlockSpec((None, n_new, KV_LORA), row3),
            pl.BlockSpec((None, n_new, ROPE), row3),
            *[pl.BlockSpec((None, page, KV_LORA), pmap(k)) for k in range(n_pages)],
            *[pl.BlockSpec((None, ROPE, page), pmap(k)) for k in range(n_pages)],
            pl.BlockSpec((N_HEADS, KV_LORA, HEAD_DIM), lambda b, j, pt: (0, 0, 0)),
        ],
        out_specs=pl.BlockSpec((None, rows, HEAD_DIM), row3),
        scratch_shapes=[pltpu.VMEM((rows, 1), F32), pltpu.VMEM((rows, 1), F32),
                        pltpu.VMEM((rows, KV_LORA), F32)],
    )
    return pl.pallas_call(
        functools.partial(_attn_decode_kernel, n_pages=n_pages, n_new=n_new, group=group),
        grid_spec=grid_spec,
        out_shape=jax.ShapeDtypeStruct((nb, rows, HEAD_DIM), F32),
        compiler_params=_params("parallel", "arbitrary"),
        name="mla_decode",
    )(page_table, ql, qr, new_c, new_r, *([cache_c] * n_pages), *([cache_r] * n_pages), w_uvt)


def _hgrn_gates(qb, fb, ib, gb, lb, keep):
    logf = jnp.log(lb + (1.0 - lb) * jax.nn.sigmoid(fb))
    kk = (1.0 - lb) * jax.nn.sigmoid(-fb)
    if keep is not None:
        logf = jnp.where(keep, logf, 0.0)
        kk = jnp.where(keep, kk, 0.0)
    return logf, kk, _silu(qb), ib, _silu(gb)


def _hgrn_kernel(qb_ref, fb_ref, ib_ref, gb_ref, lb_ref, nw_ref, s0_ref, y_ref, s_ref, st_sc,
                 *, nbt, chunk, l_valid, l_total, state_kv):
    c = pl.program_id(1)

    @pl.when(c == 0)
    def _():
        for nb in range(nbt):
            for h in range(N_HEADS):
                st_sc[nb, h] = s0_ref[nb, h].T if state_kv else s0_ref[nb, h]

    lb = lb_ref[...]
    nw = nw_ref[...]
    t_row = lax.broadcasted_iota(jnp.int32, (chunk, chunk), 0)
    t_col = lax.broadcasted_iota(jnp.int32, (chunk, chunk), 1)
    causal = t_row >= t_col
    tri = jnp.where(causal, 1.0, 0.0).astype(BF16)
    tok = lax.broadcasted_iota(jnp.int32, (chunk, 1), 0)
    keep = (c * chunk + tok < l_valid) if l_valid < l_total else None

    gates = [_hgrn_gates(qb_ref[nb], fb_ref[nb], ib_ref[nb], gb_ref[nb], lb, keep) for nb in range(nbt)]
    decay = functools.reduce(jnp.maximum, [-jnp.sum(g[0], axis=0, keepdims=True) for g in gates])
    safe = jnp.max(decay) < HGRN_SAFE_DECAY

    def heads(h):
        return slice(h * HEAD_DIM, (h + 1) * HEAD_DIM)

    @pl.when(safe)
    def _():
        for nb in range(nbt):
            logf, kk, q, v, gate = gates[nb]
            hi, mid, lo = _split3(logf)
            b = _nn(tri, hi) + _nn(tri, mid) + _nn(tri, lo)
            b_end = b[chunk - 1:chunk, :]
            qd = (q * jnp.exp(b)).astype(BF16)
            ke = (kk * jnp.exp(-b)).astype(BF16)
            wd = (kk * jnp.exp(b_end - b)).astype(BF16)
            e_end = jnp.exp(b_end)
            vb = v.astype(BF16)
            for h in range(N_HEADS):
                hs = heads(h)
                a = jnp.where(causal, _nt(qd[:, hs], ke[:, hs]), 0.0).astype(BF16)
                st = st_sc[nb, h]
                o = _nn(a, vb[:, hs]) + _nt(qd[:, hs], st.astype(BF16))
                st_sc[nb, h] = st * e_end[:, hs] + _tn(vb[:, hs], wd[:, hs])
                y_ref[nb, :, hs] = (_rms(o) * nw[:, hs] * gate[:, hs]).astype(BF16)

    @pl.when(jnp.logical_not(safe))
    def _():
        def batch_body(nb, carry):
            logf, kk, q, v, gate = _hgrn_gates(qb_ref[nb], fb_ref[nb], ib_ref[nb], gb_ref[nb], lb, keep)
            for h in range(N_HEADS):
                hs = heads(h)
                lf, qh, vh = logf[:, hs], q[:, hs], v[:, hs]
                kh = kk[:, hs].astype(BF16)

                def token(t, carry_t):
                    st, o = carry_t
                    here = tok == t
                    f_row = jnp.exp(jnp.sum(jnp.where(here, lf, 0.0), axis=0, keepdims=True))
                    st = st * f_row + _tn(jnp.where(here, vh, 0.0).astype(BF16), kh)
                    o = o + _nt(jnp.where(here, qh, 0.0).astype(BF16), st.astype(BF16))
                    return st, o

                st, o = lax.fori_loop(0, chunk, token,
                                      (st_sc[nb, h], jnp.zeros((chunk, HEAD_DIM), F32)))
                st_sc[nb, h] = st
                y_ref[nb, :, hs] = (_rms(o) * nw[:, hs] * gate[:, hs]).astype(BF16)
            return carry

        lax.fori_loop(0, nbt, batch_body, 0)

    @pl.when(c == pl.num_programs(1) - 1)
    def _():
        for nb in range(nbt):
            for h in range(N_HEADS):
                s_ref[nb, h] = st_sc[nb, h].T if state_kv else st_sc[nb, h]


def _hgrn(z3, lb, norm_w, s0, *, chunk, nbt, l_valid, state_kv):
    nb, l_total, _ = z3.shape
    assert l_total % chunk == 0 and nb % nbt == 0

    def zspec(k):
        return pl.BlockSpec((nbt, chunk, D_HALF), lambda b, c: (b, c, k))

    sshape = (nbt, N_HEADS, HEAD_DIM, HEAD_DIM)
    sspec = pl.BlockSpec(sshape, lambda b, c: (b, 0, 0, 0))
    vec = pl.BlockSpec((1, D_HALF), lambda b, c: (0, 0))
    return pl.pallas_call(
        functools.partial(_hgrn_kernel, nbt=nbt, chunk=chunk, l_valid=l_valid, l_total=l_total,
                          state_kv=state_kv),
        grid=(nb // nbt, l_total // chunk),
        in_specs=[zspec(1), zspec(2), zspec(3), zspec(4), vec, vec, sspec],
        out_specs=[pl.BlockSpec((nbt, chunk, D_HALF), lambda b, c: (b, c, 0)), sspec],
        out_shape=[jax.ShapeDtypeStruct((nb, l_total, D_HALF), BF16),
                   jax.ShapeDtypeStruct(s0.shape, F32)],
        scratch_shapes=[pltpu.VMEM(sshape, F32)],
        compiler_params=_params("parallel", "arbitrary"),
        name="hgrn",
    )(z3, z3, z3, z3, lb, norm_w, s0)


def _outproj_router_kernel(hp_ref, hs_ref, yap_ref, yas_ref, ybp_ref, ybs_ref, anw_ref, wout_ref,
                           fnw_ref, wrh_ref, wrl_ref, br_ref,
                           h2_ref, xf_ref, slab_ref, cnt_ref, *, tiles_p, tm):
    i = pl.program_id(0)
    is_p = i < tiles_p
    h = jnp.where(is_p, hp_ref[...], hs_ref[...])
    ya = jnp.where(is_p, yap_ref[...], yas_ref[...])
    yb = jnp.where(is_p, ybp_ref[...], ybs_ref[...])
    ya = (_rms(ya) * anw_ref[...]).astype(BF16)
    h2 = h + (_nn(ya, wout_ref[:D_HALF, :]) + _nn(yb, wout_ref[D_HALF:, :]))
    h2_ref[...] = h2
    xf = _rms(h2) * fnw_ref[...]
    xf_ref[...] = xf
    x_hi = xf.astype(BF16)
    x_lo = (xf - x_hi.astype(F32)).astype(BF16)
    w_hi = wrh_ref[...]
    logits = _nn(x_hi, w_hi) + _nn(x_lo, w_hi) + _nn(x_hi, wrl_ref[...]) + br_ref[...]
    lane = lax.broadcasted_iota(jnp.int32, logits.shape, 1)

    def first_max(x):
        m = jnp.max(x, axis=-1, keepdims=True)
        return m, jnp.min(jnp.where(x == m, lane, LANES), axis=-1, keepdims=True)

    gl = jnp.where(lane < N_GROUPS, logits, NEG)
    g_max, g_idx = first_max(gl)
    pg_top = 1.0 / jnp.sum(jnp.exp(gl - g_max), axis=-1, keepdims=True)
    lo = N_GROUPS + EXP_PER_GROUP * g_idx
    el = jnp.where((lane >= lo) & (lane < lo + EXP_PER_GROUP), logits, NEG)
    e1, i1 = first_max(el)
    e2, i2 = first_max(jnp.where(lane == i1, NEG, el))
    r = jnp.exp(e2 - e1)
    w1 = pg_top / (1.0 + r)
    w2 = w1 * r

    @pl.when(i == 0)
    def _():
        cnt_ref[...] = jnp.zeros_like(cnt_ref)

    onehot = jnp.where((lane == i1) | (lane == i2), 1.0, 0.0)
    t_row = lax.broadcasted_iota(jnp.int32, (tm, tm), 0)
    t_col = lax.broadcasted_iota(jnp.int32, (tm, tm), 1)
    before = jnp.where(t_row > t_col, 1.0, 0.0).astype(BF16)
    prefix = _nn(before, onehot.astype(BF16)) + cnt_ref[0:1, :]
    r1 = jnp.sum(jnp.where(lane == i1, prefix, 0.0), axis=-1, keepdims=True)
    r2 = jnp.sum(jnp.where(lane == i2, prefix, 0.0), axis=-1, keepdims=True)
    cnt_ref[...] = cnt_ref[...] + jnp.sum(onehot, axis=0, keepdims=True)
    cols = [(i1 - N_GROUPS).astype(F32), (i2 - N_GROUPS).astype(F32), w1, w2, r1, r2]
    slab = jnp.zeros(logits.shape, F32)
    for k, val in enumerate(cols):
        slab = jnp.where(lane == k, val, slab)
    slab_ref[...] = slab


def _outproj_router(h_p, h_s, ya_p, ya_s, yb_p, yb_s, a_norm_w, w_out, f_norm_w, wr_hi, wr_lo, b_r):
    n_p, n_s = h_p.shape[0], h_s.shape[0]
    tm = _pick_tile(n_s, 256)
    assert n_p % tm == 0
    tiles_p, tiles_s = n_p // tm, n_s // tm
    n = n_p + n_s
    pmap = lambda i: (jnp.minimum(i, tiles_p - 1), 0)
    smap = lambda i: (jnp.maximum(i - tiles_p, 0), 0)
    const = lambda i: (0, 0)
    row = lambda i: (i, 0)
    return pl.pallas_call(
        functools.partial(_outproj_router_kernel, tiles_p=tiles_p, tm=tm),
        grid=(tiles_p + tiles_s,),
        in_specs=[
            pl.BlockSpec((tm, D_MODEL), pmap), pl.BlockSpec((tm, D_MODEL), smap),
            pl.BlockSpec((tm, D_HALF), pmap), pl.BlockSpec((tm, D_HALF), smap),
            pl.BlockSpec((tm, D_HALF), pmap), pl.BlockSpec((tm, D_HALF), smap),
            pl.BlockSpec((1, D_HALF), const),
            pl.BlockSpec((D_MODEL, D_MODEL), const),
            pl.BlockSpec((1, D_MODEL), const),
            pl.BlockSpec((D_MODEL, LANES), const), pl.BlockSpec((D_MODEL, LANES), const),
            pl.BlockSpec((1, LANES), const),
        ],
        out_specs=[pl.BlockSpec((tm, D_MODEL), row), pl.BlockSpec((tm, D_MODEL), row),
                   pl.BlockSpec((tm, LANES), row), pl.BlockSpec((8, LANES), const)],
        out_shape=[jax.ShapeDtypeStruct((n, D_MODEL), F32), jax.ShapeDtypeStruct((n, D_MODEL), F32),
                   jax.ShapeDtypeStruct((n, LANES), F32), jax.ShapeDtypeStruct((8, LANES), F32)],
        compiler_params=_params("arbitrary"),
        name="outproj_router",
    )(h_p, h_s, ya_p, ya_s, yb_p, yb_s, a_norm_w, w_out, f_norm_w, wr_hi, wr_lo, b_r)


def _expert_kernel(te_ref, tv_ref, x_ref, wg_ref, wu_ref, wd_ref, y_ref, wg_sc, wu_sc, wd_sc):
    t = pl.program_id(0)

    @pl.when((t == 0) | (te_ref[t] != te_ref[jnp.maximum(t - 1, 0)]))
    def _():
        wg_sc[...] = wg_ref[...].astype(BF16)
        wu_sc[...] = wu_ref[...].astype(BF16)
        wd_sc[...] = wd_ref[...].astype(BF16)

    @pl.when(tv_ref[t] == 1)
    def _():
        x = x_ref[...].astype(BF16)
        hdn = (_silu(_nn(x, wg_sc[...])) * _nn(x, wu_sc[...])).astype(BF16)
        y_ref[...] = _nn(hdn, wd_sc[...])

    @pl.when(tv_ref[t] == 0)
    def _():
        y_ref[...] = jnp.zeros_like(y_ref)


def _expert_ffn(tile_expert, tile_valid, x_sorted, w_gate, w_up, w_down):
    n = x_sorted.shape[0]
    grid_spec = pltpu.PrefetchScalarGridSpec(
        num_scalar_prefetch=2,
        grid=(n // MOE_TILE,),
        in_specs=[
            pl.BlockSpec((MOE_TILE, D_MODEL), lambda t, te, tv: (t, 0)),
            pl.BlockSpec((None, D_MODEL, D_FF), lambda t, te, tv: (te[t], 0, 0)),
            pl.BlockSpec((None, D_MODEL, D_FF), lambda t, te, tv: (te[t], 0, 0)),
            pl.BlockSpec((None, D_FF, D_MODEL), lambda t, te, tv: (te[t], 0, 0)),
        ],
        out_specs=pl.BlockSpec((MOE_TILE, D_MODEL), lambda t, te, tv: (t, 0)),
        scratch_shapes=[pltpu.VMEM((D_MODEL, D_FF), BF16), pltpu.VMEM((D_MODEL, D_FF), BF16),
                        pltpu.VMEM((D_FF, D_MODEL), BF16)],
    )
    return pl.pallas_call(
        _expert_kernel,
        grid_spec=grid_spec,
        out_shape=jax.ShapeDtypeStruct((n, D_MODEL), F32),
        compiler_params=_params("arbitrary"),
        name="expert_ffn",
    )(tile_expert, tile_valid, x_sorted, w_gate, w_up, w_down)


def _final_kernel(h2_ref, y0_ref, y1_ref, slab_ref, nw_ref, op_ref, os_ref, *, tiles_p):
    i = pl.program_id(0)
    slab = slab_ref[...]
    h = h2_ref[...] + (slab[:, 2:3] * y0_ref[...] + slab[:, 3:4] * y1_ref[...])
    y = _rms(h) * nw_ref[...]

    @pl.when(i < tiles_p)
    def _():
        op_ref[...] = y

    @pl.when(i >= tiles_p)
    def _():
        os_ref[...] = y


def _final(h2, y_pair, slab, norm_w, n_p):
    n = h2.shape[0]
    n_s = n - n_p
    tm = _pick_tile(n_s, 256)
    tiles_p, tiles_s = n_p // tm, n_s // tm
    return pl.pallas_call(
        functools.partial(_final_kernel, tiles_p=tiles_p),
        grid=(tiles_p + tiles_s,),
        in_specs=[
            pl.BlockSpec((tm, D_MODEL), lambda i: (i, 0)),
            pl.BlockSpec((None, tm, D_MODEL), lambda i: (0, i, 0)),
            pl.BlockSpec((None, tm, D_MODEL), lambda i: (1, i, 0)),
            pl.BlockSpec((tm, LANES), lambda i: (i, 0)),
            pl.BlockSpec((1, D_MODEL), lambda i: (0, 0)),
        ],
        out_specs=[pl.BlockSpec((tm, D_MODEL), lambda i: (jnp.minimum(i, tiles_p - 1), 0)),
                   pl.BlockSpec((tm, D_MODEL), lambda i: (jnp.maximum(i - tiles_p, 0), 0))],
        out_shape=[jax.ShapeDtypeStruct((n_p, D_MODEL), F32), jax.ShapeDtypeStruct((n_s, D_MODEL), F32)],
        compiler_params=_params("arbitrary"),
        name="combine_final_norm",
    )(h2, y_pair, y_pair, slab, norm_w)


def _rope_table(pos):
    half = ROPE // 2
    inv_freq = jnp.power(ROPE_THETA, -jnp.arange(half, dtype=F32) / half)
    ang = pos.astype(F32)[:, None] * inv_freq[None, :]
    cos, sin = jnp.cos(ang), jnp.sin(ang)
    return jnp.concatenate([cos, cos, sin, sin], axis=-1)


def _rotate_half_cols(w):
    half = ROPE // 2
    return jnp.concatenate([-w[..., half:], w[..., :half]], axis=-1)


def kernel(x_prompt, x_sample, cache_ckv, cache_krope, state_hgrn, page_table, meta_tokens, attn_norm_w, w_in, q_norm_w, w_uq, kv_norm_w, w_uk, w_uv, mla_out_norm_w, hgrn_lb_logits, hgrn_out_norm_w, w_out, ffn_norm_w, w_router_group, b_router_group, w_router_expert, b_router_expert, w_gate, w_up, w_down, final_norm_w):
    n_batch, seq, _ = x_prompt.shape
    n_dec, n_new, _ = x_sample.shape
    depth = w_in.shape[0]
    assert depth == 1
    n_p, n_s = n_batch * seq, n_dec * n_new
    past = page_table.shape[1] * cache_ckv.shape[2]

    wi = w_in[0]
    c0 = Q_LORA + KV_LORA
    w_kr = wi[:, c0:c0 + ROPE]
    w_z = jnp.concatenate([wi[:, :c0], wi[:, c0 + ROPE:], w_kr, _rotate_half_cols(w_kr),
                           jnp.zeros((D_MODEL, LANES), F32)], axis=1).astype(BF16)
    uq = w_uq[0]
    uq_rope = uq[:, :, HEAD_DIM:]
    w_q = jnp.concatenate([uq[:, :, :HEAD_DIM].reshape(Q_LORA, D_HALF),
                           uq_rope.reshape(Q_LORA, N_HEADS * ROPE),
                           _rotate_half_cols(uq_rope).reshape(Q_LORA, N_HEADS * ROPE)], axis=1).astype(BF16)
    w_ukt = jnp.transpose(w_uk[0], (1, 2, 0)).astype(BF16)
    w_uvt = jnp.transpose(w_uv[0], (1, 0, 2)).astype(BF16)
    w_o = w_out[0].astype(BF16)
    w_r = jnp.concatenate([w_router_group[0], w_router_expert[0],
                           jnp.zeros((D_MODEL, LANES - N_GROUPS - N_EXPERTS), F32)], axis=1)
    wr_hi = w_r.astype(BF16)
    wr_lo = (w_r - wr_hi.astype(F32)).astype(BF16)
    b_r = jnp.concatenate([b_router_group[0], b_router_expert[0],
                           jnp.zeros((LANES - N_GROUPS - N_EXPERTS,), F32)])[None, :]
    lb =jax.nn.softmax(hgrn_lb_logits.astype(F32), axis=0)[0][None, :]
    a_nw, q_nw, kv_nw = attn_norm_w[0][None, :], q_norm_w[0][None, :], kv_norm_w[0][None, :]
    mo_nw, ho_nw, f_nw = mla_out_norm_w[0][None, :], hgrn_out_norm_w[0][None, :], ffn_norm_w[0][None, :]

    z_m = _in_proj(meta_tokens, a_nw, w_z)
    _, _, ckv_m, kr_m, ckvb_m, krb_m = _qkv_prep(z_m, _rope_table(jnp.arange(N_META)), q_nw, kv_nw, w_q, w_ukt)
    _, s_meta = _hgrn(z_m[None], lb, ho_nw, jnp.zeros((1, N_HEADS, HEAD_DIM, HEAD_DIM), F32),
                      chunk=N_META, nbt=1, l_valid=N_META, state_kv=False)
    meta_c = jnp.pad(ckvb_m, ((0, LANES - N_META), (0, 0)))
    meta_r = jnp.pad(krb_m, ((0, LANES - N_META), (0, 0)))

    xp = x_prompt.reshape(n_p, D_MODEL)
    z_p = _in_proj(xp, a_nw, w_z)
    cs_p = jnp.tile(_rope_table(N_META + jnp.arange(seq)), (n_batch, 1))
    ql_p, qr_p, ckv_p, kr_p, ckvb_p, krb_p = _qkv_prep(z_p, cs_p, q_nw, kv_nw, w_q, w_ukt)
    ya_p = _attn_prompt(ql_p, qr_p, ckvb_p, krb_p, meta_c, meta_r, w_uvt, n_batch)
    s0_p = jnp.broadcast_to(s_meta, (n_batch, N_HEADS, HEAD_DIM, HEAD_DIM))
    yb_p, st_p = _hgrn(z_p.reshape(n_batch, seq, Z_WIDTH), lb, ho_nw, s0_p,
                       chunk=min(64, seq), nbt=n_batch, l_valid=seq, state_kv=False)
    st_p = jnp.swapaxes(st_p, -1, -2)

    xs = x_sample.reshape(n_s, D_MODEL)
    z_s = _in_proj(xs, a_nw, w_z)
    cs_s = jnp.tile(_rope_table(past + jnp.arange(n_new)), (n_dec, 1))
    ql_s, qr_s, ckv_s, kr_s, _, _ = _qkv_prep(z_s, cs_s, q_nw, kv_nw, w_q, w_ukt)

    def dec_rows(a):
        d = a.shape[-1]
        return jnp.transpose(a.reshape(N_HEADS, n_dec, n_new, d), (1, 0, 2, 3)).reshape(n_dec, N_HEADS * n_new, d)

    ya_s = _attn_decode(dec_rows(ql_s), dec_rows(qr_s), ckv_s.reshape(n_dec, n_new, KV_LORA),
                        kr_s.reshape(n_dec, n_new, ROPE), cache_ckv[0],
                        jnp.swapaxes(cache_krope[0], 1, 2), page_table, w_uvt)
    ya_s = jnp.transpose(ya_s.reshape(n_dec, N_HEADS, n_new, HEAD_DIM), (0, 2, 1, 3)).reshape(n_s, D_HALF)
    pad_new = -n_new % 8
    z_s3 = jnp.pad(z_s.reshape(n_dec, n_new, Z_WIDTH), ((0, 0), (0, pad_new), (0, 0)))
    yb_s, st_s = _hgrn(z_s3, lb, ho_nw, state_hgrn[0].astype(F32), chunk=n_new + pad_new,
                       nbt=math.gcd(n_dec, 4), l_valid=n_new, state_kv=True)
    yb_s = yb_s[:, :n_new].reshape(n_s, D_HALF)

    h2, xf, slab, cnt = _outproj_router(xp, xs, ya_p, ya_s, yb_p.reshape(n_p, D_HALF), yb_s,
                                        mo_nw, w_o, f_nw, wr_hi, wr_lo, b_r)

    n = n_p + n_s
    e_idx = slab[:, 0:2].astype(jnp.int32)
    rank = slab[:, 4:6].astype(jnp.int32)
    counts = cnt[0, N_GROUPS:N_GROUPS + N_EXPERTS].astype(jnp.int32)
    padded = (counts + MOE_TILE - 1) // MOE_TILE * MOE_TILE
    ends = jnp.cumsum(padded)
    pos = (ends - padded)[e_idx] + rank
    n_tiles = -(-2 * n // MOE_TILE) + N_EXPERTS
    starts = jnp.arange(n_tiles, dtype=jnp.int32) * MOE_TILE
    tile_valid = (starts < ends[-1]).astype(jnp.int32)
    last_valid = jnp.maximum(ends[-1] // MOE_TILE - 1, 0)
    tile_expert = jnp.sum((ends[None, :] <= jnp.minimum(starts, last_valid * MOE_TILE)[:, None])
                          .astype(jnp.int32), axis=1)
    token_of_row = jnp.zeros((n_tiles * MOE_TILE,), jnp.int32).at[pos.reshape(-1)].set(
        jnp.repeat(jnp.arange(n, dtype=jnp.int32), 2))
    x_sorted = xf.at[token_of_row].get(mode="promise_in_bounds")
    y_sorted = _expert_ffn(tile_expert, tile_valid, x_sorted, w_gate[0], w_up[0], w_down[0])
    y_slots = y_sorted.at[pos.T.reshape(-1)].get(mode="promise_in_bounds").reshape(2, n, D_MODEL)

    y_p, y_s = _final(h2, y_slots, slab, final_norm_w[None, :], n_p)

    def with_meta(m, p, d):
        return jnp.concatenate([jnp.broadcast_to(m[None], (n_batch, N_META, d)), p.reshape(n_batch, seq, d)], axis=1)[None]

    return (y_p.reshape(n_batch, seq, D_MODEL),
            y_s.reshape(n_dec, n_new, D_MODEL),
            with_meta(ckv_m, ckv_p, KV_LORA),
            with_meta(kr_m, kr_p, ROPE),
            st_p[None].astype(x_prompt.dtype),
            ckv_s.reshape(n_dec, n_new, KV_LORA)[None],
            kr_s.reshape(n_dec, n_new, ROPE)[None],
            st_s[None].astype(state_hgrn.dtype))
```

```python
import functools
import math

import jax
import jax.numpy as jnp
from jax import lax
from jax.experimental import pallas as pl
from jax.experimental.pallas import tpu as pltpu

F32 = jnp.float32
BF16 = jnp.bfloat16

D_MODEL = 2048
N_META = 16
N_HEADS = 8
HEAD_DIM = 128
ROPE = 64
Q_LORA = 512
KV_LORA = 512
D_HALF = N_HEADS * HEAD_DIM
ROPE_THETA = 10000.0
SM_SCALE = (HEAD_DIM + ROPE) ** -0.5
EPS = 1e-6
N_GROUPS = 4
EXP_PER_GROUP = 8
N_EXPERTS = N_GROUPS * EXP_PER_GROUP
D_FF = 512

Z_WIDTH = 5376
Z_TILE = 1792
Z_ROPE_BLOCK = 40

LANES = 128
VMEM_LIMIT = 56 * 1024 * 1024
NEG = -1e30
HGRN_SAFE_DECAY = 60.0
MOE_TILE = 256
X_SLABS = D_MODEL // LANES
PAGES_PER_CHUNK = 32


def _nn(a, b):
    return jnp.dot(a, b, preferred_element_type=F32)


def _nt(a, b):
    return lax.dot_general(a, b, (((1,), (1,)), ((), ())), preferred_element_type=F32)


def _tn(a, b):
    return lax.dot_general(a, b, (((0,), (0,)), ((), ())), preferred_element_type=F32)


def _split3(x):
    hi = x.astype(BF16)
    r1 = x - hi.astype(F32)
    mid = r1.astype(BF16)
    lo = (r1 - mid.astype(F32)).astype(BF16)
    return hi, mid, lo


def _rms(x):
    return x * lax.rsqrt(jnp.mean(x * x, axis=-1, keepdims=True) + EPS)


def _silu(x):
    return x * jax.nn.sigmoid(x)


def _pick_tile(n, pref):
    if n <= pref:
        return n
    for t in range(pref, 7, -1):
        if n % t == 0 and t % 8 == 0:
            return t
    raise ValueError(f"no tile for {n}")


def _params(*sem):
    return pltpu.CompilerParams(dimension_semantics=sem, vmem_limit_bytes=VMEM_LIMIT)


def _inproj_kernel(x_ref, nw_ref, w_ref, z_ref, xn_ref):
    @pl.when(pl.program_id(1) == 0)
    def _():
        xn_ref[...] = (_rms(x_ref[...]) * nw_ref[...]).astype(BF16)

    z_ref[...] = _nn(xn_ref[...], w_ref[...])


def _in_proj(x, norm_w, w_z):
    n = x.shape[0]
    tm = _pick_tile(n, 512)
    return pl.pallas_call(
        _inproj_kernel,
        grid=(n // tm, Z_WIDTH // Z_TILE),
        in_specs=[
            pl.BlockSpec((tm, D_MODEL), lambda i, j: (i, 0)),
            pl.BlockSpec((1, D_MODEL), lambda i, j: (0, 0)),
            pl.BlockSpec((D_MODEL, Z_TILE), lambda i, j: (0, j)),
        ],
        out_specs=pl.BlockSpec((tm, Z_TILE), lambda i, j: (i, j)),
        out_shape=jax.ShapeDtypeStruct((n, Z_WIDTH), F32),
        scratch_shapes=[pltpu.VMEM((tm, D_MODEL), BF16)],
        compiler_params=_params("parallel", "arbitrary"),
        name="in_proj",
    )(x, norm_w, w_z)


def _prep_kernel(zq_ref, zkv_ref, zkr_ref, cs_ref, qnw_ref, kvnw_ref, wq_ref, wuk_ref,
                 ql_ref, qr_ref, ckv_ref, kr_ref, ckvb_ref, krb_ref):
    cqn = (_rms(zq_ref[...]) * qnw_ref[...]).astype(BF16)
    q = _nn(cqn, wq_ref[...])
    cs = cs_ref[...]
    cos = cs[:, :ROPE]
    sin = cs[:, ROPE:]
    for h in range(N_HEADS):
        qn = q[:, h * HEAD_DIM:(h + 1) * HEAD_DIM].astype(BF16)
        ql_ref[h] = (_nn(qn, wuk_ref[h]) * SM_SCALE).astype(BF16)
        a = q[:, D_HALF + h * ROPE:D_HALF + (h + 1) * ROPE]
        b = q[:, D_HALF + N_HEADS * ROPE + h * ROPE:D_HALF + N_HEADS * ROPE + (h + 1) * ROPE]
        qr_ref[h] = ((a * cos + b * sin) * SM_SCALE).astype(BF16)
    ckv = _rms(zkv_ref[...]) * kvnw_ref[...]
    ckv_ref[...] = ckv
    ckvb_ref[...] = ckv.astype(BF16)
    prod = zkr_ref[...] * cs
    kr = prod[:, :ROPE] + prod[:, ROPE:]
    kr_ref[...] = kr
    krb_ref[...] = kr.astype(BF16)


def _qkv_prep(z, cs, q_norm_w, kv_norm_w, w_q, w_ukt):
    n = z.shape[0]
    tm = _pick_tile(n, 256)
    const2 = lambda i: (0, 0)
    return pl.pallas_call(
        _prep_kernel,
        grid=(n // tm,),
        in_specs=[
            pl.BlockSpec((tm, Q_LORA), lambda i: (i, 0)),
            pl.BlockSpec((tm, KV_LORA), lambda i: (i, 1)),
            pl.BlockSpec((tm, LANES), lambda i: (i, Z_ROPE_BLOCK)),
            pl.BlockSpec((tm, LANES), lambda i: (i, 0)),
            pl.BlockSpec((1, Q_LORA), const2),
            pl.BlockSpec((1, KV_LORA), const2),
            pl.BlockSpec((Q_LORA, 2 * D_HALF), const2),
            pl.BlockSpec((N_HEADS, HEAD_DIM, KV_LORA), lambda i: (0, 0, 0)),
        ],
        out_specs=[
            pl.BlockSpec((N_HEADS, tm, KV_LORA), lambda i: (0, i, 0)),
            pl.BlockSpec((N_HEADS, tm, ROPE), lambda i: (0, i, 0)),
            pl.BlockSpec((tm, KV_LORA), lambda i: (i, 0)),
            pl.BlockSpec((tm, ROPE), lambda i: (i, 0)),
            pl.BlockSpec((tm, KV_LORA), lambda i: (i, 0)),
            pl.BlockSpec((tm, ROPE), lambda i: (i, 0)),
        ],
        out_shape=[
            jax.ShapeDtypeStruct((N_HEADS, n, KV_LORA), BF16),
            jax.ShapeDtypeStruct((N_HEADS, n, ROPE), BF16),
            jax.ShapeDtypeStruct((n, KV_LORA), F32),
            jax.ShapeDtypeStruct((n, ROPE), F32),
            jax.ShapeDtypeStruct((n, KV_LORA), BF16),
            jax.ShapeDtypeStruct((n, ROPE), BF16),
        ],
        compiler_params=_params("parallel"),
        name="qkv_prep",
    )(z, z, z, cs, q_norm_w, kv_norm_w, w_q, w_ukt)


def _attn_prompt_kernel(ql_ref, qr_ref, kc_ref, kr_ref, mc_ref, mr_ref, wuv_ref, o_ref,
                        m_sc, l_sc, acc_sc, *, tq, tk):
    qi = pl.program_id(1)
    kj = pl.program_id(2)
    rows = N_HEADS * tq
    ql = ql_ref[...].reshape(rows, KV_LORA)
    qr = qr_ref[...].reshape(rows, ROPE)

    @pl.when(kj == 0)
    def _():
        mc = mc_ref[...]
        s = _nt(ql, mc) + _nt(qr, mr_ref[...])
        col = lax.broadcasted_iota(jnp.int32, s.shape, 1)
        s = jnp.where(col < N_META, s, NEG)
        m = jnp.max(s, axis=-1, keepdims=True)
        p = jnp.exp(s - m)
        m_sc[...] = m
        l_sc[...] = jnp.sum(p, axis=-1, keepdims=True)
        acc_sc[...] = _nn(p.astype(BF16), mc)

    def key_block(masked):
        kc = kc_ref[...]
        s = _nt(ql, kc) + _nt(qr, kr_ref[...])
        if masked:
            row = lax.broadcasted_iota(jnp.int32, s.shape, 0)
            col = lax.broadcasted_iota(jnp.int32, s.shape, 1)
            s = jnp.where(kj * tk + col <= qi * tq + (row & (tq - 1)), s, NEG)
        m_prev = m_sc[...]
        m_new = jnp.maximum(m_prev, jnp.max(s, axis=-1, keepdims=True))
        alpha = jnp.exp(m_prev - m_new)
        p = jnp.exp(s - m_new)
        l_sc[...] = alpha * l_sc[...] + jnp.sum(p, axis=-1, keepdims=True)
        acc_sc[...] = alpha * acc_sc[...] + _nn(p.astype(BF16), kc)
        m_sc[...] = m_new

    last_key = kj * tk + (tk - 1)

    @pl.when(last_key <= qi * tq)
    def _():
        key_block(False)

    @pl.when((last_key > qi * tq) & (kj * tk <= qi * tq + (tq - 1)))
    def _():
        key_block(True)

    @pl.when(kj == pl.num_programs(2) - 1)
    def _():
        o = (acc_sc[...] / l_sc[...]).astype(BF16)
        for h in range(N_HEADS):
            o_ref[:, h * HEAD_DIM:(h + 1) * HEAD_DIM] = _nn(o[h * tq:(h + 1) * tq], wuv_ref[h])


def _attn_prompt(ql, qr, ckv_b, kr_b, meta_c, meta_r, w_uvt, n_batch):
    n = ckv_b.shape[0]
    seq = n // n_batch
    tq = 128
    tk = min(512, seq)
    assert seq % tq == 0 and seq % tk == 0 and tq & (tq - 1) == 0
    qb, kb = seq // tq, seq // tk
    rows = N_HEADS * tq

    def kmap(b, i, j):
        return (b * kb + jnp.minimum(j, (i * tq + tq - 1) // tk), 0)

    return pl.pallas_call(
        functools.partial(_attn_prompt_kernel, tq=tq, tk=tk),
        grid=(n_batch, qb, kb),
        in_specs=[
            pl.BlockSpec((N_HEADS, tq, KV_LORA), lambda b, i, j: (0, b * qb + i, 0)),
            pl.BlockSpec((N_HEADS, tq, ROPE), lambda b, i, j: (0, b * qb + i, 0)),
            pl.BlockSpec((tk, KV_LORA), kmap),
            pl.BlockSpec((tk, ROPE), kmap),
            pl.BlockSpec((LANES, KV_LORA), lambda b, i, j: (0, 0)),
            pl.BlockSpec((LANES, ROPE), lambda b, i, j: (0, 0)),
            pl.BlockSpec((N_HEADS, KV_LORA, HEAD_DIM), lambda b, i, j: (0, 0, 0)),
        ],
        out_specs=pl.BlockSpec((tq, D_HALF), lambda b, i, j: (b * qb + i, 0)),
        out_shape=jax.ShapeDtypeStruct((n, D_HALF), F32),
        scratch_shapes=[pltpu.VMEM((rows, 1), F32), pltpu.VMEM((rows, 1), F32),
                        pltpu.VMEM((rows, KV_LORA), F32)],
        compiler_params=_params("parallel", "parallel", "arbitrary"),
        name="mla_prompt",
    )(ql, qr, ckv_b, kr_b, meta_c, meta_r, w_uvt)


def _attn_decode_kernel(pt_ref, ql_ref, qr_ref, nc_ref, nr_ref, cc_hbm, cr_hbm, wuv_ref, o_ref,
                        cbuf, rbuf, sem, kc_sc, kr_sc, m_sc, l_sc, acc_sc,
                        *, chunk, n_chunks, n_new, page):
    b = pl.program_id(0)
    n_batch = pl.num_programs(0)
    rows = N_HEADS * n_new
    ql = ql_ref[...]
    qr = qr_ref[...]

    def page_copies(bb, c, k, slot):
        pg = pt_ref[bb, c * chunk + k]
        return (pltpu.make_async_copy(cc_hbm.at[pg], cbuf.at[slot, k], sem.at[slot]),
                pltpu.make_async_copy(cr_hbm.at[pg], rbuf.at[slot, k], sem.at[slot]))

    def start_pages(bb, c, slot, ks):
        for k in ks:
            for cp in page_copies(bb, c, k, slot):
                cp.start()

    @pl.when(b == 0)
    def _():
        start_pages(0, 0, 0, range(chunk))

    qlf = ql.astype(F32)
    qrf = qr.astype(F32)
    nc = nc_ref[...]
    nr = nr_ref[...]
    tok = lax.rem(lax.broadcasted_iota(jnp.int32, (rows, 1), 0), n_new)
    cols = []
    for t in range(n_new):
        sc = (jnp.sum(qlf * nc[t:t + 1, :], axis=-1, keepdims=True)
              + jnp.sum(qrf * nr[t:t + 1, :], axis=-1, keepdims=True))
        cols.append(jnp.where(tok >= t, sc, NEG))
    m0 = functools.reduce(jnp.maximum, cols)
    ps = [jnp.exp(c - m0) for c in cols]
    m_sc[...] = m0
    l_sc[...] = functools.reduce(jnp.add, ps)
    acc_sc[...] = functools.reduce(jnp.add, [p * nc[t:t + 1, :] for t, p in enumerate(ps)])

    def chunk_pair(i, carry):
        for slot in range(2):
            c = 2 * i + slot
            for k in range(chunk):
                for cp in page_copies(b, c, k, slot):
                    cp.wait()
            last = c + 1 == n_chunks
            next_b = jnp.where(last, jnp.minimum(b + 1, n_batch - 1), b)
            next_c = jnp.where(last, 0, c + 1)
            for k in range(chunk):
                start_pages(next_b, next_c, 1 - slot, (k,))
                kc_sc[k * page:(k + 1) * page, :] = cbuf[slot, k].astype(BF16)
                kr_sc[:, k * page:(k + 1) * page] = rbuf[slot, k].astype(BF16)
            kc = kc_sc[...]
            s = _nt(ql, kc) + _nn(qr, kr_sc[...])
            m_prev = m_sc[...]
            m_new = jnp.maximum(m_prev, jnp.max(s, axis=-1, keepdims=True))
            alpha = jnp.exp(m_prev - m_new)
            p = jnp.exp(s - m_new)
            l_sc[...] = alpha * l_sc[...] + jnp.sum(p, axis=-1, keepdims=True)
            acc_sc[...] = alpha * acc_sc[...] + _nn(p.astype(BF16), kc)
            m_sc[...] = m_new
        return carry

    lax.fori_loop(0, n_chunks // 2, chunk_pair, 0)

    @pl.when(b == n_batch - 1)
    def _():
        for k in range(chunk):
            for cp in page_copies(b, 0, k, 0):
                cp.wait()

    o = (acc_sc[...] / l_sc[...]).astype(BF16)
    head = lax.broadcasted_iota(jnp.int32, (rows, 1), 0) // n_new
    y = jnp.zeros((rows, HEAD_DIM), F32)
    for h in range(N_HEADS):
        y = jnp.where(head == h, _nn(o, wuv_ref[h]), y)
    o_ref[...] = y


def _attn_decode(ql, qr, new_c, new_r, cache_c, cache_r, page_table, w_uvt):
    nb, rows, _ = ql.shape
    n_new = rows // N_HEADS
    total_pages = page_table.shape[1]
    chunk = min(PAGES_PER_CHUNK, total_pages // 2)
    n_chunks = total_pages // chunk
    assert total_pages % chunk == 0 and n_chunks % 2 == 0
    page = cache_c.shape[1]

    row3 = lambda b, pt: (b, 0, 0)
    grid_spec = pltpu.PrefetchScalarGridSpec(
        num_scalar_prefetch=1,
        grid=(nb,),
        in_specs=[
            pl.BlockSpec((None, rows, KV_LORA), row3),
            pl.BlockSpec((None, rows, ROPE), row3),
            pl.BlockSpec((None, n_new, KV_LORA), row3),
            pl.BlockSpec((None, n_new, ROPE), row3),
            pl.BlockSpec(memory_space=pl.ANY),
            pl.BlockSpec(memory_space=pl.ANY),
            pl.BlockSpec((N_HEADS, KV_LORA, HEAD_DIM), lambda b, pt: (0, 0, 0)),
        ],
        out_specs=pl.BlockSpec((None, rows, HEAD_DIM), row3),
        scratch_shapes=[pltpu.VMEM((2, chunk, page, KV_LORA), F32),
                        pltpu.VMEM((2, chunk, ROPE, page), F32),
                        pltpu.SemaphoreType.DMA((2,)),
                        pltpu.VMEM((chunk * page, KV_LORA), BF16),
                        pltpu.VMEM((ROPE, chunk * page), BF16),
                        pltpu.VMEM((rows, 1), F32), pltpu.VMEM((rows, 1), F32),
                        pltpu.VMEM((rows, KV_LORA), F32)],
    )
    return pl.pallas_call(
        functools.partial(_attn_decode_kernel, chunk=chunk, n_chunks=n_chunks, n_new=n_new, page=page),
        grid_spec=grid_spec,
        out_shape=jax.ShapeDtypeStruct((nb, rows, HEAD_DIM), F32),
        compiler_params=_params("arbitrary"),
        name="mla_decode",
    )(page_table, ql, qr, new_c, new_r, cache_c, cache_r, w_uvt)


def _hgrn_gates(qb, fb, ib, gb, lb, keep):
    logf = jnp.log(lb + (1.0 - lb) * jax.nn.sigmoid(fb))
    kk = (1.0 - lb) * jax.nn.sigmoid(-fb)
    if keep is not None:
        logf = jnp.where(keep, logf, 0.0)
        kk = jnp.where(keep, kk, 0.0)
    return logf, kk, _silu(qb), ib, _silu(gb)


def _hgrn_kernel(qb_ref, fb_ref, ib_ref, gb_ref, lb_ref, nw_ref, s0_ref, y_ref, s_ref, st_sc,
                 *, nbt, chunk, l_valid, l_total, state_kv):
    c = pl.program_id(1)

    @pl.when(c == 0)
    def _():
        for nb in range(nbt):
            for h in range(N_HEADS):
                st_sc[nb, h] = s0_ref[nb, h].T if state_kv else s0_ref[nb, h]

    lb = lb_ref[...]
    nw = nw_ref[...]
    t_row = lax.broadcasted_iota(jnp.int32, (chunk, chunk), 0)
    t_col = lax.broadcasted_iota(jnp.int32, (chunk, chunk), 1)
    causal = t_row >= t_col
    tri = jnp.where(causal, 1.0, 0.0).astype(BF16)
    tok = lax.broadcasted_iota(jnp.int32, (chunk, 1), 0)
    keep = (c * chunk + tok < l_valid) if l_valid < l_total else None

    gates = [_hgrn_gates(qb_ref[nb], fb_ref[nb], ib_ref[nb], gb_ref[nb], lb, keep) for nb in range(nbt)]
    decay = functools.reduce(jnp.maximum, [-jnp.sum(g[0], axis=0, keepdims=True) for g in gates])
    safe = jnp.max(decay) < HGRN_SAFE_DECAY

    def heads(h):
        return slice(h * HEAD_DIM, (h + 1) * HEAD_DIM)

    @pl.when(safe)
    def _():
        for nb in range(nbt):
            logf, kk, q, v, gate = gates[nb]
            hi, mid, lo = _split3(logf)
            b = _nn(tri, hi) + _nn(tri, mid) + _nn(tri, lo)
            b_end = b[chunk - 1:chunk, :]
            qd = (q * jnp.exp(b)).astype(BF16)
            ke = (kk * jnp.exp(-b)).astype(BF16)
            wd = (kk * jnp.exp(b_end - b)).astype(BF16)
            e_end = jnp.exp(b_end)
            vb = v.astype(BF16)
            for h in range(N_HEADS):
                hs = heads(h)
                a = jnp.where(causal, _nt(qd[:, hs], ke[:, hs]), 0.0).astype(BF16)
                st = st_sc[nb, h]
                o = _nn(a, vb[:, hs]) + _nt(qd[:, hs], st.astype(BF16))
                st_sc[nb, h] = st * e_end[:, hs] + _tn(vb[:, hs], wd[:, hs])
                y_ref[nb, :, hs] = (_rms(o) * nw[:, hs] * gate[:, hs]).astype(BF16)

    @pl.when(jnp.logical_not(safe))
    def _():
        def batch_body(nb, carry):
            logf, kk, q, v, gate = _hgrn_gates(qb_ref[nb], fb_ref[nb], ib_ref[nb], gb_ref[nb], lb, keep)
            for h in range(N_HEADS):
                hs = heads(h)
                lf, qh, vh = logf[:, hs], q[:, hs], v[:, hs]
                kh = kk[:, hs].astype(BF16)

                def token(t, carry_t):
                    st, o = carry_t
                    here = tok == t
                    f_row = jnp.exp(jnp.sum(jnp.where(here, lf, 0.0), axis=0, keepdims=True))
                    st = st * f_row + _tn(jnp.where(here, vh, 0.0).astype(BF16), kh)
                    o = o + _nt(jnp.where(here, qh, 0.0).astype(BF16), st.astype(BF16))
                    return st, o

                st, o = lax.fori_loop(0, chunk, token,
                                      (st_sc[nb, h], jnp.zeros((chunk, HEAD_DIM), F32)))
                st_sc[nb, h] = st
                y_ref[nb, :, hs] = (_rms(o) * nw[:, hs] * gate[:, hs]).astype(BF16)
            return carry

        lax.fori_loop(0, nbt, batch_body, 0)

    @pl.when(c == pl.num_programs(1) - 1)
    def _():
        for nb in range(nbt):
            for h in range(N_HEADS):
                s_ref[nb, h] = st_sc[nb, h].T if state_kv else st_sc[nb, h]


def _hgrn(z3, lb, norm_w, s0, *, chunk, nbt, l_valid, state_kv):
    nb, l_total, _ = z3.shape
    assert l_total % chunk == 0 and nb % nbt == 0

    def zspec(k):
        return pl.BlockSpec((nbt, chunk, D_HALF), lambda b, c: (b, c, k))

    sshape = (nbt, N_HEADS, HEAD_DIM, HEAD_DIM)
    sspec = pl.BlockSpec(sshape, lambda b, c: (b, 0, 0, 0))
    vec = pl.BlockSpec((1, D_HALF), lambda b, c: (0, 0))
    return pl.pallas_call(
        functools.partial(_hgrn_kernel, nbt=nbt, chunk=chunk, l_valid=l_valid, l_total=l_total,
                          state_kv=state_kv),
        grid=(nb // nbt, l_total // chunk),
        in_specs=[zspec(1), zspec(2), zspec(3), zspec(4), vec, vec, sspec],
        out_specs=[pl.BlockSpec((nbt, chunk, D_HALF), lambda b, c: (b, c, 0)), sspec],
        out_shape=[jax.ShapeDtypeStruct((nb, l_total, D_HALF), BF16),
                   jax.ShapeDtypeStruct(s0.shape, F32)],
        scratch_shapes=[pltpu.VMEM(sshape, F32)],
        compiler_params=_params("parallel", "arbitrary"),
        name="hgrn",
    )(z3, z3, z3, z3, lb, norm_w, s0)


def _outproj_router_kernel(hp_ref, hs_ref, yap_ref, yas_ref, ybp_ref, ybs_ref, anw_ref, wout_ref,
                           fnw_ref, wrh_ref, wrl_ref, br_ref,
                           h2_ref, xf_ref, slab_ref, cnt_ref, *, tiles_p, tm):
    i = pl.program_id(0)
    is_p = i < tiles_p
    h = jnp.where(is_p, hp_ref[...], hs_ref[...])
    ya = jnp.where(is_p, yap_ref[...], yas_ref[...])
    yb = jnp.where(is_p, ybp_ref[...], ybs_ref[...])
    ya = (_rms(ya) * anw_ref[...]).astype(BF16)
    h2 = h + (_nn(ya, wout_ref[:D_HALF, :]) + _nn(yb, wout_ref[D_HALF:, :]))
    h2_ref[...] = h2
    xf = _rms(h2) * fnw_ref[...]
    xf_ref[...] = xf
    x_hi = xf.astype(BF16)
    x_lo = (xf - x_hi.astype(F32)).astype(BF16)
    w_hi = wrh_ref[...]
    logits = _nn(x_hi, w_hi) + _nn(x_lo, w_hi) + _nn(x_hi, wrl_ref[...]) + br_ref[...]
    lane = lax.broadcasted_iota(jnp.int32, logits.shape, 1)

    def first_max(x):
        m = jnp.max(x, axis=-1, keepdims=True)
        return m, jnp.min(jnp.where(x == m, lane, LANES), axis=-1, keepdims=True)

    gl = jnp.where(lane < N_GROUPS, logits, NEG)
    g_max, g_idx = first_max(gl)
    pg_top = 1.0 / jnp.sum(jnp.exp(gl - g_max), axis=-1, keepdims=True)
    lo = N_GROUPS + EXP_PER_GROUP * g_idx
    el = jnp.where((lane >= lo) & (lane < lo + EXP_PER_GROUP), logits, NEG)
    e1, i1 = first_max(el)
    e2, i2 = first_max(jnp.where(lane == i1, NEG, el))
    r = jnp.exp(e2 - e1)
    w1 = pg_top / (1.0 + r)
    w2 = w1 * r

    @pl.when(i == 0)
    def _():
        cnt_ref[...] = jnp.zeros_like(cnt_ref)

    onehot = jnp.where((lane == i1) | (lane == i2), 1.0, 0.0)
    t_row = lax.broadcasted_iota(jnp.int32, (tm, tm), 0)
    t_col = lax.broadcasted_iota(jnp.int32, (tm, tm), 1)
    before = jnp.where(t_row > t_col, 1.0, 0.0).astype(BF16)
    prefix = _nn(before, onehot.astype(BF16)) + cnt_ref[0:1, :]
    r1 = jnp.sum(jnp.where(lane == i1, prefix, 0.0), axis=-1, keepdims=True)
    r2 = jnp.sum(jnp.where(lane == i2, prefix, 0.0), axis=-1, keepdims=True)
    cnt_ref[...] = cnt_ref[...] + jnp.sum(onehot, axis=0, keepdims=True)
    cols = [(i1 - N_GROUPS).astype(F32), (i2 - N_GROUPS).astype(F32), w1, w2, r1, r2]
    slab = jnp.zeros(logits.shape, F32)
    for k, val in enumerate(cols):
        slab = jnp.where(lane == k, val, slab)
    slab_ref[...] = slab


def _outproj_router(h_p, h_s, ya_p, ya_s, yb_p, yb_s, a_norm_w, w_out, f_norm_w, wr_hi, wr_lo, b_r):
    n_p, n_s = h_p.shape[0], h_s.shape[0]
    tm = _pick_tile(n_s, 256)
    assert n_p % tm == 0
    tiles_p, tiles_s = n_p // tm, n_s // tm
    n = n_p + n_s
    pmap = lambda i: (jnp.minimum(i, tiles_p - 1), 0)
    smap = lambda i: (jnp.maximum(i - tiles_p, 0), 0)
    const = lambda i: (0, 0)
    row = lambda i: (i, 0)
    return pl.pallas_call(
        functools.partial(_outproj_router_kernel, tiles_p=tiles_p, tm=tm),
        grid=(tiles_p + tiles_s,),
        in_specs=[
            pl.BlockSpec((tm, D_MODEL), pmap), pl.BlockSpec((tm, D_MODEL), smap),
            pl.BlockSpec((tm, D_HALF), pmap), pl.BlockSpec((tm, D_HALF), smap),
            pl.BlockSpec((tm, D_HALF), pmap), pl.BlockSpec((tm, D_HALF), smap),
            pl.BlockSpec((1, D_HALF), const),
            pl.BlockSpec((D_MODEL, D_MODEL), const),
            pl.BlockSpec((1, D_MODEL), const),
            pl.BlockSpec((D_MODEL, LANES), const), pl.BlockSpec((D_MODEL, LANES), const),
            pl.BlockSpec((1, LANES), const),
        ],
        out_specs=[pl.BlockSpec((tm, D_MODEL), row), pl.BlockSpec((tm, D_MODEL), row),
                   pl.BlockSpec((tm, LANES), row), pl.BlockSpec((8, LANES), const)],
        out_shape=[jax.ShapeDtypeStruct((n, D_MODEL), F32), jax.ShapeDtypeStruct((n, D_MODEL), F32),
                   jax.ShapeDtypeStruct((n, LANES), F32), jax.ShapeDtypeStruct((8, LANES), F32)],
        compiler_params=_params("arbitrary"),
        name="outproj_router",
    )(h_p, h_s, ya_p, ya_s, yb_p, yb_s, a_norm_w, w_out, f_norm_w, wr_hi, wr_lo, b_r)


def _expert_kernel(te_ref, tv_ref, tok_ref, x_hbm, wg_ref, wu_ref, wd_ref, y_ref,
                   xbuf, sem, wg_sc, wu_sc, wd_sc):
    t = pl.program_id(0)
    n_tiles = pl.num_programs(0)
    slot = lax.rem(t, 2)

    def row_copy(tile, r, s):
        tok = tok_ref[tile * MOE_TILE + r]
        return pltpu.make_async_copy(x_hbm.at[tok], xbuf.at[s, pl.ds(r * X_SLABS, X_SLABS)], sem.at[s])

    def start_tile(tile, s):
        def body(r, carry):
            row_copy(tile, r, s).start()
            return carry
        lax.fori_loop(0, MOE_TILE, body, 0, unroll=8)

    @pl.when(t == 0)
    def _():
        start_tile(0, 0)

    nxt = jnp.minimum(t + 1, n_tiles - 1)

    @pl.when((t + 1 < n_tiles) & (tv_ref[nxt] == 1))
    def _():
        start_tile(nxt, 1 - slot)

    @pl.when((t == 0) | (te_ref[t] != te_ref[jnp.maximum(t - 1, 0)]))
    def _():
        wg_sc[...] = wg_ref[...].astype(BF16)
        wu_sc[...] = wu_ref[...].astype(BF16)
        wd_sc[...] = wd_ref[...].astype(BF16)

    @pl.when(tv_ref[t] == 1)
    def _():
        def body(r, carry):
            row_copy(t, r, slot).wait()
            return carry
        lax.fori_loop(0, MOE_TILE, body, 0, unroll=8)
        x = jnp.concatenate([xbuf[slot, pl.ds(s, MOE_TILE, stride=X_SLABS), :] for s in range(X_SLABS)],
                            axis=1).astype(BF16)
        hdn = (_silu(_nn(x, wg_sc[...])) * _nn(x, wu_sc[...])).astype(BF16)
        y_ref[...] = _nn(hdn, wd_sc[...])

    @pl.when(tv_ref[t] == 0)
    def _():
        y_ref[...] = jnp.zeros_like(y_ref)


def _expert_ffn(tile_expert, tile_valid, token_of_row, xf3, w_gate, w_up, w_down):
    n_rows = token_of_row.shape[0]
    grid_spec = pltpu.PrefetchScalarGridSpec(
        num_scalar_prefetch=3,
        grid=(n_rows // MOE_TILE,),
        in_specs=[
            pl.BlockSpec(memory_space=pl.ANY),
            pl.BlockSpec((None, D_MODEL, D_FF), lambda t, te, tv, tok: (te[t], 0, 0)),
            pl.BlockSpec((None, D_MODEL, D_FF), lambda t, te, tv, tok: (te[t], 0, 0)),
            pl.BlockSpec((None, D_FF, D_MODEL), lambda t, te, tv, tok: (te[t], 0, 0)),
        ],
        out_specs=pl.BlockSpec((MOE_TILE, D_MODEL), lambda t, te, tv, tok: (t, 0)),
        scratch_shapes=[pltpu.VMEM((2, MOE_TILE * X_SLABS, LANES), F32),
                        pltpu.SemaphoreType.DMA((2,)),
                        pltpu.VMEM((D_MODEL, D_FF), BF16), pltpu.VMEM((D_MODEL, D_FF), BF16),
                        pltpu.VMEM((D_FF, D_MODEL), BF16)],
    )
    return pl.pallas_call(
        _expert_kernel,
        grid_spec=grid_spec,
        out_shape=jax.ShapeDtypeStruct((n_rows, D_MODEL), F32),
        compiler_params=_params("arbitrary"),
        name="expert_ffn",
    )(tile_expert, tile_valid, token_of_row, xf3, w_gate, w_up, w_down)


def _final_kernel(h2_ref, y0_ref, y1_ref, slab_ref, nw_ref, op_ref, os_ref, *, tiles_p):
    i = pl.program_id(0)
    slab = slab_ref[...]
    h = h2_ref[...] + (slab[:, 2:3] * y0_ref[...] + slab[:, 3:4] * y1_ref[...])
    y = _rms(h) * nw_ref[...]

    @pl.when(i < tiles_p)
    def _():
        op_ref[...] = y

    @pl.when(i >= tiles_p)
    def _():
        os_ref[...] = y


def _final(h2, y_pair, slab, norm_w, n_p):
    n = h2.shape[0]
    n_s = n - n_p
    tm = _pick_tile(n_s, 256)
    tiles_p, tiles_s = n_p // tm, n_s // tm
    return pl.pallas_call(
        functools.partial(_final_kernel, tiles_p=tiles_p),
        grid=(tiles_p + tiles_s,),
        in_specs=[
            pl.BlockSpec((tm, D_MODEL), lambda i: (i, 0)),
            pl.BlockSpec((None, tm, D_MODEL), lambda i: (0, i, 0)),
            pl.BlockSpec((None, tm, D_MODEL), lambda i: (1, i, 0)),
            pl.BlockSpec((tm, LANES), lambda i: (i, 0)),
            pl.BlockSpec((1, D_MODEL), lambda i: (0, 0)),
        ],
        out_specs=[pl.BlockSpec((tm, D_MODEL), lambda i: (jnp.minimum(i, tiles_p - 1), 0)),
                   pl.BlockSpec((tm, D_MODEL), lambda i: (jnp.maximum(i - tiles_p, 0), 0))],
        out_shape=[jax.ShapeDtypeStruct((n_p, D_MODEL), F32), jax.ShapeDtypeStruct((n_s, D_MODEL), F32)],
        compiler_params=_params("arbitrary"),
        name="combine_final_norm",
    )(h2, y_pair, y_pair, slab, norm_w)


def _rope_table(pos):
    half = ROPE // 2
    inv_freq = jnp.power(ROPE_THETA, -jnp.arange(half, dtype=F32) / half)
    ang = pos.astype(F32)[:, None] * inv_freq[None, :]
    cos, sin = jnp.cos(ang), jnp.sin(ang)
    return jnp.concatenate([cos, cos, sin, sin], axis=-1)


def _rotate_half_cols(w):
    half = ROPE // 2
    return jnp.concatenate([-w[..., half:], w[..., :half]], axis=-1)


def kernel(x_prompt, x_sample, cache_ckv, cache_krope, state_hgrn, page_table, meta_tokens, attn_norm_w, w_in, q_norm_w, w_uq, kv_norm_w, w_uk, w_uv, mla_out_norm_w, hgrn_lb_logits, hgrn_out_norm_w, w_out, ffn_norm_w, w_router_group, b_router_group, w_router_expert, b_router_expert, w_gate, w_up, w_down, final_norm_w):
    n_batch, seq, _ = x_prompt.shape
    n_dec, n_new, _ = x_sample.shape
    depth = w_in.shape[0]
    assert depth == 1
    n_p, n_s = n_batch * seq, n_dec * n_new
    past = page_table.shape[1] * cache_ckv.shape[2]

    wi = w_in[0]
    c0 = Q_LORA + KV_LORA
    w_kr = wi[:, c0:c0 + ROPE]
    w_z = jnp.concatenate([wi[:, :c0], wi[:, c0 + ROPE:], w_kr, _rotate_half_cols(w_kr),
                           jnp.zeros((D_MODEL, LANES), F32)], axis=1).astype(BF16)
    uq = w_uq[0]
    uq_rope = uq[:, :, HEAD_DIM:]
    w_q = jnp.concatenate([uq[:, :, :HEAD_DIM].reshape(Q_LORA, D_HALF),
                           uq_rope.reshape(Q_LORA, N_HEADS * ROPE),
                           _rotate_half_cols(uq_rope).reshape(Q_LORA, N_HEADS * ROPE)], axis=1).astype(BF16)
    w_ukt = jnp.transpose(w_uk[0], (1, 2, 0)).astype(BF16)
    w_uvt = jnp.transpose(w_uv[0], (1, 0, 2)).astype(BF16)
    w_o = w_out[0].astype(BF16)
    w_r = jnp.concatenate([w_router_group[0], w_router_expert[0],
                           jnp.zeros((D_MODEL, LANES - N_GROUPS - N_EXPERTS), F32)], axis=1)
    wr_hi = w_r.astype(BF16)
    wr_lo = (w_r - wr_hi.astype(F32)).astype(BF16)
    b_r = jnp.concatenate([b_router_group[0], b_router_expert[0],
                           jnp.zeros((LANES - N_GROUPS - N_EXPERTS,), F32)])[None, :]
    lb =jax.nn.softmax(hgrn_lb_logits.astype(F32), axis=0)[0][None, :]
    a_nw, q_nw, kv_nw = attn_norm_w[0][None, :], q_norm_w[0][None, :], kv_norm_w[0][None, :]
    mo_nw, ho_nw, f_nw = mla_out_norm_w[0][None, :], hgrn_out_norm_w[0][None, :], ffn_norm_w[0][None, :]

    z_m = _in_proj(meta_tokens, a_nw, w_z)
    _, _, ckv_m, kr_m, ckvb_m, krb_m = _qkv_prep(z_m, _rope_table(jnp.arange(N_META)), q_nw, kv_nw, w_q, w_ukt)
    _, s_meta = _hgrn(z_m[None], lb, ho_nw, jnp.zeros((1, N_HEADS, HEAD_DIM, HEAD_DIM), F32),
                      chunk=N_META, nbt=1, l_valid=N_META, state_kv=False)
    meta_c = jnp.pad(ckvb_m, ((0, LANES - N_META), (0, 0)))
    meta_r = jnp.pad(krb_m, ((0, LANES - N_META), (0, 0)))

    xp = x_prompt.reshape(n_p, D_MODEL)
    z_p = _in_proj(xp, a_nw, w_z)
    cs_p = jnp.tile(_rope_table(N_META + jnp.arange(seq)), (n_batch, 1))
    ql_p, qr_p, ckv_p, kr_p, ckvb_p, krb_p = _qkv_prep(z_p, cs_p, q_nw, kv_nw, w_q, w_ukt)
    ya_p = _attn_prompt(ql_p, qr_p, ckvb_p, krb_p, meta_c, meta_r, w_uvt, n_batch)
    s0_p = jnp.broadcast_to(s_meta, (n_batch, N_HEADS, HEAD_DIM, HEAD_DIM))
    yb_p, st_p = _hgrn(z_p.reshape(n_batch, seq, Z_WIDTH), lb, ho_nw, s0_p,
                       chunk=min(64, seq), nbt=n_batch, l_valid=seq, state_kv=False)
    st_p = jnp.swapaxes(st_p, -1, -2)

    xs = x_sample.reshape(n_s, D_MODEL)
    z_s = _in_proj(xs, a_nw, w_z)
    cs_s = jnp.tile(_rope_table(past + jnp.arange(n_new)), (n_dec, 1))
    ql_s, qr_s, ckv_s, kr_s, _, _ = _qkv_prep(z_s, cs_s, q_nw, kv_nw, w_q, w_ukt)

    def dec_rows(a):
        d = a.shape[-1]
        return jnp.transpose(a.reshape(N_HEADS, n_dec, n_new, d), (1, 0, 2, 3)).reshape(n_dec, N_HEADS * n_new, d)

    ya_s = _attn_decode(dec_rows(ql_s), dec_rows(qr_s), ckv_s.reshape(n_dec, n_new, KV_LORA),
                        kr_s.reshape(n_dec, n_new, ROPE), cache_ckv[0],
                        jnp.swapaxes(cache_krope[0], 1, 2), page_table, w_uvt)
    ya_s = jnp.transpose(ya_s.reshape(n_dec, N_HEADS, n_new, HEAD_DIM), (0, 2, 1, 3)).reshape(n_s, D_HALF)
    pad_new = -n_new % 8
    z_s3 = jnp.pad(z_s.reshape(n_dec, n_new, Z_WIDTH), ((0, 0), (0, pad_new), (0, 0)))
    yb_s, st_s = _hgrn(z_s3, lb, ho_nw, state_hgrn[0].astype(F32), chunk=n_new + pad_new,
                       nbt=math.gcd(n_dec, 4), l_valid=n_new, state_kv=True)
    yb_s = yb_s[:, :n_new].reshape(n_s, D_HALF)

    h2, xf, slab, cnt = _outproj_router(xp, xs, ya_p, ya_s, yb_p.reshape(n_p, D_HALF), yb_s,
                                        mo_nw, w_o, f_nw, wr_hi, wr_lo, b_r)

    n = n_p + n_s
    e_idx = slab[:, 0:2].astype(jnp.int32)
    rank = slab[:, 4:6].astype(jnp.int32)
    counts = cnt[0, N_GROUPS:N_GROUPS + N_EXPERTS].astype(jnp.int32)
    padded = (counts + MOE_TILE - 1) // MOE_TILE * MOE_TILE
    ends = jnp.cumsum(padded)
    pos = (ends - padded)[e_idx] + rank
    n_tiles = -(-2 * n // MOE_TILE) + N_EXPERTS
    starts = jnp.arange(n_tiles, dtype=jnp.int32) * MOE_TILE
    tile_valid = (starts < ends[-1]).astype(jnp.int32)
    last_valid = jnp.maximum(ends[-1] // MOE_TILE - 1, 0)
    tile_expert = jnp.sum((ends[None, :] <= jnp.minimum(starts, last_valid * MOE_TILE)[:, None])
                          .astype(jnp.int32), axis=1)
    token_of_row = jnp.zeros((n_tiles * MOE_TILE,), jnp.int32).at[pos.reshape(-1)].set(
        jnp.repeat(jnp.arange(n, dtype=jnp.int32), 2))
    y_sorted = _expert_ffn(tile_expert, tile_valid, token_of_row, xf.reshape(n, X_SLABS, LANES),
                           w_gate[0], w_up[0], w_down[0])
    y_slots = y_sorted.at[pos.T.reshape(-1)].get(mode="promise_in_bounds").reshape(2, n, D_MODEL)

    y_p, y_s = _final(h2, y_slots, slab, final_norm_w[None, :], n_p)

    def with_meta(m, p, d):
        return jnp.concatenate([jnp.broadcast_to(m[None], (n_batch, N_META, d)), p.reshape(n_batch, seq, d)], axis=1)[None]

    return (y_p.reshape(n_batch, seq, D_MODEL),
            y_s.reshape(n_dec, n_new, D_MODEL),
            with_meta(ckv_m, ckv_p, KV_LORA),
            with_meta(kr_m, kr_p, ROPE),
            st_p[None].astype(x_prompt.dtype),
            ckv_s.reshape(n_dec, n_new, KV_LORA)[None],
            kr_s.reshape(n_dec, n_new, ROPE)[None],
            st_s[None].astype(state_hgrn.dtype))
```

```python
import functools
import math

import jax
import jax.numpy as jnp
from jax import lax
from jax.experimental import pallas as pl
from jax.experimental.pallas import tpu as pltpu

F32 = jnp.float32
BF16 = jnp.bfloat16

D_MODEL = 2048
N_META = 16
N_HEADS = 8
HEAD_DIM = 128
ROPE = 64
Q_LORA = 512
KV_LORA = 512
D_HALF = N_HEADS * HEAD_DIM
ROPE_THETA = 10000.0
SM_SCALE = (HEAD_DIM + ROPE) ** -0.5
EPS = 1e-6
N_GROUPS = 4
EXP_PER_GROUP = 8
N_EXPERTS = N_GROUPS * EXP_PER_GROUP
D_FF = 512

Z_WIDTH = 5376
Z_TILE = 1792
Z_ROPE_BLOCK = 40

LANES = 128
VMEM_LIMIT = 56 * 1024 * 1024
NEG = -1e30
HGRN_SAFE_DECAY = 60.0
MOE_TILE = 256
X_SLABS = D_MODEL // LANES
GATHER_UNROLL = 8
PAGES_PER_CHUNK = 32
DECODE_SLOTS = 3


def _nn(a, b):
    return jnp.dot(a, b, preferred_element_type=F32)


def _nt(a, b):
    return lax.dot_general(a, b, (((1,), (1,)), ((), ())), preferred_element_type=F32)


def _tn(a, b):
    return lax.dot_general(a, b, (((0,), (0,)), ((), ())), preferred_element_type=F32)


def _split3(x):
    hi = x.astype(BF16)
    r1 = x - hi.astype(F32)
    mid = r1.astype(BF16)
    lo = (r1 - mid.astype(F32)).astype(BF16)
    return hi, mid, lo


def _rms(x):
    return x * lax.rsqrt(jnp.mean(x * x, axis=-1, keepdims=True) + EPS)


def _silu(x):
    return x * jax.nn.sigmoid(x)


def _pick_tile(n, pref):
    if n <= pref:
        return n
    for t in range(pref, 7, -1):
        if n % t == 0 and t % 8 == 0:
            return t
    raise ValueError(f"no tile for {n}")


def _params(*sem):
    return pltpu.CompilerParams(dimension_semantics=sem, vmem_limit_bytes=VMEM_LIMIT)


def _inproj_kernel(x_ref, nw_ref, w_ref, z_ref, xn_ref):
    @pl.when(pl.program_id(1) == 0)
    def _():
        xn_ref[...] = (_rms(x_ref[...]) * nw_ref[...]).astype(BF16)

    z_ref[...] = _nn(xn_ref[...], w_ref[...])


def _in_proj(x, norm_w, w_z):
    n = x.shape[0]
    tm = _pick_tile(n, 512)
    return pl.pallas_call(
        _inproj_kernel,
        grid=(n // tm, Z_WIDTH // Z_TILE),
        in_specs=[
            pl.BlockSpec((tm, D_MODEL), lambda i, j: (i, 0)),
            pl.BlockSpec((1, D_MODEL), lambda i, j: (0, 0)),
            pl.BlockSpec((D_MODEL, Z_TILE), lambda i, j: (0, j)),
        ],
        out_specs=pl.BlockSpec((tm, Z_TILE), lambda i, j: (i, j)),
        out_shape=jax.ShapeDtypeStruct((n, Z_WIDTH), F32),
        scratch_shapes=[pltpu.VMEM((tm, D_MODEL), BF16)],
        compiler_params=_params("parallel", "arbitrary"),
        name="in_proj",
    )(x, norm_w, w_z)


def _prep_kernel(zq_ref, zkv_ref, zkr_ref, cs_ref, qnw_ref, kvnw_ref, wq_ref, wuk_ref,
                 ql_ref, qr_ref, ckv_ref, kr_ref, ckvb_ref, krb_ref):
    cqn = (_rms(zq_ref[...]) * qnw_ref[...]).astype(BF16)
    q = _nn(cqn, wq_ref[...])
    cs = cs_ref[...]
    cos = cs[:, :ROPE]
    sin = cs[:, ROPE:]
    for h in range(N_HEADS):
        qn = q[:, h * HEAD_DIM:(h + 1) * HEAD_DIM].astype(BF16)
        ql_ref[h] = (_nn(qn, wuk_ref[h]) * SM_SCALE).astype(BF16)
        a = q[:, D_HALF + h * ROPE:D_HALF + (h + 1) * ROPE]
        b = q[:, D_HALF + N_HEADS * ROPE + h * ROPE:D_HALF + N_HEADS * ROPE + (h + 1) * ROPE]
        qr_ref[h] = ((a * cos + b * sin) * SM_SCALE).astype(BF16)
    ckv = _rms(zkv_ref[...]) * kvnw_ref[...]
    ckv_ref[...] = ckv
    ckvb_ref[...] = ckv.astype(BF16)
    prod = zkr_ref[...] * cs
    kr = prod[:, :ROPE] + prod[:, ROPE:]
    kr_ref[...] = kr
    krb_ref[...] = kr.astype(BF16)


def _qkv_prep(z, cs, q_norm_w, kv_norm_w, w_q, w_ukt):
    n = z.shape[0]
    tm = _pick_tile(n, 256)
    const2 = lambda i: (0, 0)
    return pl.pallas_call(
        _prep_kernel,
        grid=(n // tm,),
        in_specs=[
            pl.BlockSpec((tm, Q_LORA), lambda i: (i, 0)),
            pl.BlockSpec((tm, KV_LORA), lambda i: (i, 1)),
            pl.BlockSpec((tm, LANES), lambda i: (i, Z_ROPE_BLOCK)),
            pl.BlockSpec((tm, LANES), lambda i: (i, 0)),
            pl.BlockSpec((1, Q_LORA), const2),
            pl.BlockSpec((1, KV_LORA), const2),
            pl.BlockSpec((Q_LORA, 2 * D_HALF), const2),
            pl.BlockSpec((N_HEADS, HEAD_DIM, KV_LORA), lambda i: (0, 0, 0)),
        ],
        out_specs=[
            pl.BlockSpec((N_HEADS, tm, KV_LORA), lambda i: (0, i, 0)),
            pl.BlockSpec((N_HEADS, tm, ROPE), lambda i: (0, i, 0)),
            pl.BlockSpec((tm, KV_LORA), lambda i: (i, 0)),
            pl.BlockSpec((tm, ROPE), lambda i: (i, 0)),
            pl.BlockSpec((tm, KV_LORA), lambda i: (i, 0)),
            pl.BlockSpec((tm, ROPE), lambda i: (i, 0)),
        ],
        out_shape=[
            jax.ShapeDtypeStruct((N_HEADS, n, KV_LORA), BF16),
            jax.ShapeDtypeStruct((N_HEADS, n, ROPE), BF16),
            jax.ShapeDtypeStruct((n, KV_LORA), F32),
            jax.ShapeDtypeStruct((n, ROPE), F32),
            jax.ShapeDtypeStruct((n, KV_LORA), BF16),
            jax.ShapeDtypeStruct((n, ROPE), BF16),
        ],
        compiler_params=_params("parallel"),
        name="qkv_prep",
    )(z, z, z, cs, q_norm_w, kv_norm_w, w_q, w_ukt)


def _attn_prompt_kernel(ql_ref, qr_ref, kc_ref, kr_ref, wuv_ref, o_ref, *, tq, tk, n_blocks):
    qi = pl.program_id(1)
    rows = N_HEADS * tq
    ql = ql_ref[...].reshape(rows, KV_LORA)
    qr = qr_ref[...].reshape(rows, ROPE)
    needed = (qi * tq + (tq - 1)) // tk + 1

    def prefix(nblk):
        width = LANES + nblk * tk
        kc = kc_ref[0:width, :]
        s = _nt(ql, kc) + _nt(qr, kr_ref[0:width, :])
        s_meta = s[:, :LANES]
        s_meta = jnp.where(lax.broadcasted_iota(jnp.int32, s_meta.shape, 1) < N_META, s_meta, NEG)
        s_last = s[:, width - tk:]
        row = lax.broadcasted_iota(jnp.int32, s_last.shape, 0)
        col = lax.broadcasted_iota(jnp.int32, s_last.shape, 1)
        s_last = jnp.where((nblk - 1) * tk + col <= qi * tq + (row & (tq - 1)), s_last, NEG)
        middle = [s[:, LANES:width - tk]] if nblk > 1 else []
        s = jnp.concatenate([s_meta] + middle + [s_last], axis=1)
        p = jnp.exp(s - jnp.max(s, axis=-1, keepdims=True))
        o = (_nn(p.astype(BF16), kc) / jnp.sum(p, axis=-1, keepdims=True)).astype(BF16)
        for h in range(N_HEADS):
            o_ref[:, h * HEAD_DIM:(h + 1) * HEAD_DIM] = _nn(o[h * tq:(h + 1) * tq], wuv_ref[h])

    for nblk in range(1, n_blocks + 1):
        pl.when(needed == nblk)(functools.partial(prefix, nblk))


def _attn_prompt(ql, qr, keys_c, keys_r, w_uvt, n_batch):
    seq = keys_c.shape[1] - LANES
    n = n_batch * seq
    tq = 128
    tk = min(512, seq)
    assert seq % tq == 0 and seq % tk == 0 and tq & (tq - 1) == 0
    qb = seq // tq
    return pl.pallas_call(
        functools.partial(_attn_prompt_kernel, tq=tq, tk=tk, n_blocks=seq // tk),
        grid=(n_batch, qb),
        in_specs=[
            pl.BlockSpec((N_HEADS, tq, KV_LORA), lambda b, i: (0, b * qb + i, 0)),
            pl.BlockSpec((N_HEADS, tq, ROPE), lambda b, i: (0, b * qb + i, 0)),
            pl.BlockSpec((None, LANES + seq, KV_LORA), lambda b, i: (b, 0, 0)),
            pl.BlockSpec((None, LANES + seq, ROPE), lambda b, i: (b, 0, 0)),
            pl.BlockSpec((N_HEADS, KV_LORA, HEAD_DIM), lambda b, i: (0, 0, 0)),
        ],
        out_specs=pl.BlockSpec((tq, D_HALF), lambda b, i: (b * qb + i, 0)),
        out_shape=jax.ShapeDtypeStruct((n, D_HALF), F32),
        compiler_params=_params("parallel", "parallel"),
        name="mla_prompt",
    )(ql, qr, keys_c, keys_r, w_uvt)


def _attn_decode_kernel(pt_ref, ql_ref, qr_ref, nc_ref, nr_ref, cc_hbm, cr_hbm, wuv_ref, o_ref,
                        cbuf, rbuf, sem, kc_sc, kr_sc, m_sc, l_sc, acc_sc,
                        *, chunk, n_chunks, n_batch, n_new, page):
    b = pl.program_id(0)
    total = n_batch * n_chunks
    ahead = DECODE_SLOTS - 1
    rows = N_HEADS * n_new
    ql = ql_ref[...]
    qr = qr_ref[...]

    def chunk_copies(g):
        bb = lax.div(g, jnp.int32(n_chunks))
        first_page = (g - bb * n_chunks) * chunk
        slot = lax.rem(g, jnp.int32(DECODE_SLOTS))
        copies = []
        for k in range(chunk):
            pg = pt_ref[bb, first_page + k]
            copies.append(pltpu.make_async_copy(cc_hbm.at[pg], cbuf.at[slot, k], sem.at[slot]))
            copies.append(pltpu.make_async_copy(cr_hbm.at[pg], rbuf.at[slot, k], sem.at[slot]))
        return copies

    @pl.when(b == 0)
    def _():
        for g in range(min(ahead, total)):
            for cp in chunk_copies(jnp.int32(g)):
                cp.start()

    qlf = ql.astype(F32)
    qrf = qr.astype(F32)
    nc = nc_ref[...]
    nr = nr_ref[...]
    tok = lax.rem(lax.broadcasted_iota(jnp.int32, (rows, 1), 0), n_new)
    cols = []
    for t in range(n_new):
        sc = (jnp.sum(qlf * nc[t:t + 1, :], axis=-1, keepdims=True)
              + jnp.sum(qrf * nr[t:t + 1, :], axis=-1, keepdims=True))
        cols.append(jnp.where(tok >= t, sc, NEG))
    m0 = functools.reduce(jnp.maximum, cols)
    ps = [jnp.exp(c - m0) for c in cols]
    m_sc[...] = m0
    l_sc[...] = functools.reduce(jnp.add, ps)
    acc_sc[...] = functools.reduce(jnp.add, [p * nc[t:t + 1, :] for t, p in enumerate(ps)])

    def chunk_body(c, carry):
        g = b * n_chunks + c
        slot = lax.rem(g, jnp.int32(DECODE_SLOTS))
        for cp in chunk_copies(g):
            cp.wait()

        @pl.when(g + ahead < total)
        def _():
            for cp in chunk_copies(g + ahead):
                cp.start()

        for k in range(chunk):
            kc_sc[k * page:(k + 1) * page, :] = cbuf[slot, k].astype(BF16)
            kr_sc[:, k * page:(k + 1) * page] = rbuf[slot, k].astype(BF16)
        kc = kc_sc[...]
        s = _nt(ql, kc) + _nn(qr, kr_sc[...])
        m_prev = m_sc[...]
        m_new = jnp.maximum(m_prev, jnp.max(s, axis=-1, keepdims=True))
        alpha = jnp.exp(m_prev - m_new)
        p = jnp.exp(s - m_new)
        l_sc[...] = alpha * l_sc[...] + jnp.sum(p, axis=-1, keepdims=True)
        acc_sc[...] = alpha * acc_sc[...] + _nn(p.astype(BF16), kc)
        m_sc[...] = m_new
        return carry

    lax.fori_loop(0, n_chunks, chunk_body, 0)

    o = (acc_sc[...] / l_sc[...]).astype(BF16)
    head = lax.broadcasted_iota(jnp.int32, (rows, 1), 0) // n_new
    y = jnp.zeros((rows, HEAD_DIM), F32)
    for h in range(N_HEADS):
        y = jnp.where(head == h, _nn(o, wuv_ref[h]), y)
    o_ref[...] = y


def _attn_decode(ql, qr, new_c, new_r, cache_c, cache_r, page_table, w_uvt):
    nb, rows, _ = ql.shape
    n_new = rows // N_HEADS
    total_pages = page_table.shape[1]
    chunk = min(PAGES_PER_CHUNK, total_pages)
    n_chunks = total_pages // chunk
    assert total_pages % chunk == 0
    page = cache_c.shape[1]

    row3 = lambda b, pt: (b, 0, 0)
    grid_spec = pltpu.PrefetchScalarGridSpec(
        num_scalar_prefetch=1,
        grid=(nb,),
        in_specs=[
            pl.BlockSpec((None, rows, KV_LORA), row3),
            pl.BlockSpec((None, rows, ROPE), row3),
            pl.BlockSpec((None, n_new, KV_LORA), row3),
            pl.BlockSpec((None, n_new, ROPE), row3),
            pl.BlockSpec(memory_space=pl.ANY),
            pl.BlockSpec(memory_space=pl.ANY),
            pl.BlockSpec((N_HEADS, KV_LORA, HEAD_DIM), lambda b, pt: (0, 0, 0)),
        ],
        out_specs=pl.BlockSpec((None, rows, HEAD_DIM), row3),
        scratch_shapes=[pltpu.VMEM((DECODE_SLOTS, chunk, page, KV_LORA), F32),
                        pltpu.VMEM((DECODE_SLOTS, chunk, ROPE, page), F32),
                        pltpu.SemaphoreType.DMA((DECODE_SLOTS,)),
                        pltpu.VMEM((chunk * page, KV_LORA), BF16),
                        pltpu.VMEM((ROPE, chunk * page), BF16),
                        pltpu.VMEM((rows, 1), F32), pltpu.VMEM((rows, 1), F32),
                        pltpu.VMEM((rows, KV_LORA), F32)],
    )
    return pl.pallas_call(
        functools.partial(_attn_decode_kernel, chunk=chunk, n_chunks=n_chunks, n_batch=nb, n_new=n_new,
                          page=page),
        grid_spec=grid_spec,
        out_shape=jax.ShapeDtypeStruct((nb, rows, HEAD_DIM), F32),
        compiler_params=_params("arbitrary"),
        name="mla_decode",
    )(page_table, ql, qr, new_c, new_r, cache_c, cache_r, w_uvt)


def _hgrn_gates(qb, fb, ib, gb, lb, keep):
    logf = jnp.log(lb + (1.0 - lb) * jax.nn.sigmoid(fb))
    kk = (1.0 - lb) * jax.nn.sigmoid(-fb)
    if keep is not None:
        logf = jnp.where(keep, logf, 0.0)
        kk = jnp.where(keep, kk, 0.0)
    return logf, kk, _silu(qb), ib, _silu(gb)


def _hgrn_kernel(qb_ref, fb_ref, ib_ref, gb_ref, lb_ref, nw_ref, s0_ref, y_ref, s_ref, st_sc,
                 *, nbt, chunk, l_valid, l_total, state_kv):
    c = pl.program_id(1)

    @pl.when(c == 0)
    def _():
        for nb in range(nbt):
            for h in range(N_HEADS):
                st_sc[nb, h] = s0_ref[nb, h].T if state_kv else s0_ref[nb, h]

    lb = lb_ref[...]
    nw = nw_ref[...]
    t_row = lax.broadcasted_iota(jnp.int32, (chunk, chunk), 0)
    t_col = lax.broadcasted_iota(jnp.int32, (chunk, chunk), 1)
    causal = t_row >= t_col
    tri = jnp.where(causal, 1.0, 0.0).astype(BF16)
    tok = lax.broadcasted_iota(jnp.int32, (chunk, 1), 0)
    keep = (c * chunk + tok < l_valid) if l_valid < l_total else None

    gates = [_hgrn_gates(qb_ref[nb], fb_ref[nb], ib_ref[nb], gb_ref[nb], lb, keep) for nb in range(nbt)]
    decay = functools.reduce(jnp.maximum, [-jnp.sum(g[0], axis=0, keepdims=True) for g in gates])
    safe = jnp.max(decay) < HGRN_SAFE_DECAY

    def heads(h):
        return slice(h * HEAD_DIM, (h + 1) * HEAD_DIM)

    @pl.when(safe)
    def _():
        for nb in range(nbt):
            logf, kk, q, v, gate = gates[nb]
            hi, mid, lo = _split3(logf)
            b = _nn(tri, hi) + _nn(tri, mid) + _nn(tri, lo)
            b_end = b[chunk - 1:chunk, :]
            qd = (q * jnp.exp(b)).astype(BF16)
            ke = (kk * jnp.exp(-b)).astype(BF16)
            wd = (kk * jnp.exp(b_end - b)).astype(BF16)
            e_end = jnp.exp(b_end)
            vb = v.astype(BF16)
            for h in range(N_HEADS):
                hs = heads(h)
                a = jnp.where(causal, _nt(qd[:, hs], ke[:, hs]), 0.0).astype(BF16)
                st = st_sc[nb, h]
                o = _nn(a, vb[:, hs]) + _nt(qd[:, hs], st.astype(BF16))
                st_sc[nb, h] = st * e_end[:, hs] + _tn(vb[:, hs], wd[:, hs])
                y_ref[nb, :, hs] = (_rms(o) * nw[:, hs] * gate[:, hs]).astype(BF16)

    @pl.when(jnp.logical_not(safe))
    def _():
        def batch_body(nb, carry):
            logf, kk, q, v, gate = _hgrn_gates(qb_ref[nb], fb_ref[nb], ib_ref[nb], gb_ref[nb], lb, keep)
            for h in range(N_HEADS):
                hs = heads(h)
                lf, qh, vh = logf[:, hs], q[:, hs], v[:, hs]
                kh = kk[:, hs].astype(BF16)

                def token(t, carry_t):
                    st, o = carry_t
                    here = tok == t
                    f_row = jnp.exp(jnp.sum(jnp.where(here, lf, 0.0), axis=0, keepdims=True))
                    st = st * f_row + _tn(jnp.where(here, vh, 0.0).astype(BF16), kh)
                    o = o + _nt(jnp.where(here, qh, 0.0).astype(BF16), st.astype(BF16))
                    return st, o

                st, o = lax.fori_loop(0, chunk, token,
                                      (st_sc[nb, h], jnp.zeros((chunk, HEAD_DIM), F32)))
                st_sc[nb, h] = st
                y_ref[nb, :, hs] = (_rms(o) * nw[:, hs] * gate[:, hs]).astype(BF16)
            return carry

        lax.fori_loop(0, nbt, batch_body, 0)

    @pl.when(c == pl.num_programs(1) - 1)
    def _():
        for nb in range(nbt):
            for h in range(N_HEADS):
                s_ref[nb, h] = st_sc[nb, h].T if state_kv else st_sc[nb, h]


def _hgrn(z3, lb, norm_w, s0, *, chunk, nbt, l_valid, state_kv):
    nb, l_total, _ = z3.shape
    assert l_total % chunk == 0 and nb % nbt == 0

    def zspec(k):
        return pl.BlockSpec((nbt, chunk, D_HALF), lambda b, c: (b, c, k))

    sshape = (nbt, N_HEADS, HEAD_DIM, HEAD_DIM)
    sspec = pl.BlockSpec(sshape, lambda b, c: (b, 0, 0, 0))
    vec = pl.BlockSpec((1, D_HALF), lambda b, c: (0, 0))
    return pl.pallas_call(
        functools.partial(_hgrn_kernel, nbt=nbt, chunk=chunk, l_valid=l_valid, l_total=l_total,
                          state_kv=state_kv),
        grid=(nb // nbt, l_total // chunk),
        in_specs=[zspec(1), zspec(2), zspec(3), zspec(4), vec, vec, sspec],
        out_specs=[pl.BlockSpec((nbt, chunk, D_HALF), lambda b, c: (b, c, 0)), sspec],
        out_shape=[jax.ShapeDtypeStruct((nb, l_total, D_HALF), BF16),
                   jax.ShapeDtypeStruct(s0.shape, F32)],
        scratch_shapes=[pltpu.VMEM(sshape, F32)],
        compiler_params=_params("parallel", "arbitrary"),
        name="hgrn",
    )(z3, z3, z3, z3, lb, norm_w, s0)


def _outproj_router_kernel(hp_ref, hs_ref, yap_ref, yas_ref, ybp_ref, ybs_ref, anw_ref, wout_ref,
                           fnw_ref, wrh_ref, wrl_ref, br_ref,
                           h2_ref, xf_ref, slab_ref, cnt_ref, *, tiles_p, tm):
    i = pl.program_id(0)
    is_p = i < tiles_p
    h = jnp.where(is_p, hp_ref[...], hs_ref[...])
    ya = jnp.where(is_p, yap_ref[...], yas_ref[...])
    yb = jnp.where(is_p, ybp_ref[...], ybs_ref[...])
    ya = (_rms(ya) * anw_ref[...]).astype(BF16)
    h2 = h + (_nn(ya, wout_ref[:D_HALF, :]) + _nn(yb, wout_ref[D_HALF:, :]))
    h2_ref[...] = h2
    xf = _rms(h2) * fnw_ref[...]
    xf_ref[...] = xf
    x_hi = xf.astype(BF16)
    x_lo = (xf - x_hi.astype(F32)).astype(BF16)
    w_hi = wrh_ref[...]
    logits = _nn(x_hi, w_hi) + _nn(x_lo, w_hi) + _nn(x_hi, wrl_ref[...]) + br_ref[...]
    lane = lax.broadcasted_iota(jnp.int32, logits.shape, 1)

    def first_max(x):
        m = jnp.max(x, axis=-1, keepdims=True)
        return m, jnp.min(jnp.where(x == m, lane, LANES), axis=-1, keepdims=True)

    gl = jnp.where(lane < N_GROUPS, logits, NEG)
    g_max, g_idx = first_max(gl)
    pg_top = 1.0 / jnp.sum(jnp.exp(gl - g_max), axis=-1, keepdims=True)
    lo = N_GROUPS + EXP_PER_GROUP * g_idx
    el = jnp.where((lane >= lo) & (lane < lo + EXP_PER_GROUP), logits, NEG)
    e1, i1 = first_max(el)
    e2, i2 = first_max(jnp.where(lane == i1, NEG, el))
    r = jnp.exp(e2 - e1)
    w1 = pg_top / (1.0 + r)
    w2 = w1 * r

    @pl.when(i == 0)
    def _():
        cnt_ref[...] = jnp.zeros_like(cnt_ref)

    onehot = jnp.where((lane == i1) | (lane == i2), 1.0, 0.0)
    t_row = lax.broadcasted_iota(jnp.int32, (tm, tm), 0)
    t_col = lax.broadcasted_iota(jnp.int32, (tm, tm), 1)
    before = jnp.where(t_row > t_col, 1.0, 0.0).astype(BF16)
    prefix = _nn(before, onehot.astype(BF16)) + cnt_ref[0:1, :]
    r1 = jnp.sum(jnp.where(lane == i1, prefix, 0.0), axis=-1, keepdims=True)
    r2 = jnp.sum(jnp.where(lane == i2, prefix, 0.0), axis=-1, keepdims=True)
    cnt_ref[...] = cnt_ref[...] + jnp.sum(onehot, axis=0, keepdims=True)
    cols = [(i1 - N_GROUPS).astype(F32), (i2 - N_GROUPS).astype(F32), w1, w2, r1, r2]
    slab = jnp.zeros(logits.shape, F32)
    for k, val in enumerate(cols):
        slab = jnp.where(lane == k, val, slab)
    slab_ref[...] = slab


def _outproj_router(h_p, h_s, ya_p, ya_s, yb_p, yb_s, a_norm_w, w_out, f_norm_w, wr_hi, wr_lo, b_r):
    n_p, n_s = h_p.shape[0], h_s.shape[0]
    tm = _pick_tile(n_s, 256)
    assert n_p % tm == 0
    tiles_p, tiles_s = n_p // tm, n_s // tm
    n = n_p + n_s
    pmap = lambda i: (jnp.minimum(i, tiles_p - 1), 0)
    smap = lambda i: (jnp.maximum(i - tiles_p, 0), 0)
    const = lambda i: (0, 0)
    row = lambda i: (i, 0)
    return pl.pallas_call(
        functools.partial(_outproj_router_kernel, tiles_p=tiles_p, tm=tm),
        grid=(tiles_p + tiles_s,),
        in_specs=[
            pl.BlockSpec((tm, D_MODEL), pmap), pl.BlockSpec((tm, D_MODEL), smap),
            pl.BlockSpec((tm, D_HALF), pmap), pl.BlockSpec((tm, D_HALF), smap),
            pl.BlockSpec((tm, D_HALF), pmap), pl.BlockSpec((tm, D_HALF), smap),
            pl.BlockSpec((1, D_HALF), const),
            pl.BlockSpec((D_MODEL, D_MODEL), const),
            pl.BlockSpec((1, D_MODEL), const),
            pl.BlockSpec((D_MODEL, LANES), const), pl.BlockSpec((D_MODEL, LANES), const),
            pl.BlockSpec((1, LANES), const),
        ],
        out_specs=[pl.BlockSpec((tm, D_MODEL), row), pl.BlockSpec((tm, D_MODEL), row),
                   pl.BlockSpec((tm, LANES), row), pl.BlockSpec((8, LANES), const)],
        out_shape=[jax.ShapeDtypeStruct((n, D_MODEL), F32), jax.ShapeDtypeStruct((n, D_MODEL), F32),
                   jax.ShapeDtypeStruct((n, LANES), F32), jax.ShapeDtypeStruct((8, LANES), F32)],
        compiler_params=_params("arbitrary"),
        name="outproj_router",
    )(h_p, h_s, ya_p, ya_s, yb_p, yb_s, a_norm_w, w_out, f_norm_w, wr_hi, wr_lo, b_r)


def _expert_kernel(te_ref, tv_ref, tok_ref, x_hbm, wg_ref, wu_ref, wd_ref, y_ref,
                   xbuf, sem, wg_sc, wu_sc, wd_sc):
    t = pl.program_id(0)
    n_tiles = pl.num_programs(0)
    slot = lax.rem(t, 2)

    def row_copy(tile, r, s):
        tok = tok_ref[tile * MOE_TILE + r]
        return pltpu.make_async_copy(x_hbm.at[tok], xbuf.at[s, pl.ds(r * X_SLABS, X_SLABS)], sem.at[s])

    def start_tile(tile, s):
        def body(i, carry):
            for j in range(GATHER_UNROLL):
                row_copy(tile, i * GATHER_UNROLL + j, s).start(priority=j % 2)
            return carry
        lax.fori_loop(0, MOE_TILE // GATHER_UNROLL, body, 0)

    @pl.when(t == 0)
    def _():
        start_tile(0, 0)

    nxt = jnp.minimum(t + 1, n_tiles - 1)

    @pl.when((t + 1 < n_tiles) & (tv_ref[nxt] == 1))
    def _():
        start_tile(nxt, 1 - slot)

    @pl.when((t == 0) | (te_ref[t] != te_ref[jnp.maximum(t - 1, 0)]))
    def _():
        wg_sc[...] = wg_ref[...].astype(BF16)
        wu_sc[...] = wu_ref[...].astype(BF16)
        wd_sc[...] = wd_ref[...].astype(BF16)

    @pl.when(tv_ref[t] == 1)
    def _():
        def body(r, carry):
            row_copy(t, r, slot).wait()
            return carry
        lax.fori_loop(0, MOE_TILE, body, 0, unroll=8)
        x = jnp.concatenate([xbuf[slot, pl.ds(s, MOE_TILE, stride=X_SLABS), :] for s in range(X_SLABS)],
                            axis=1).astype(BF16)
        hdn = (_silu(_nn(x, wg_sc[...])) * _nn(x, wu_sc[...])).astype(BF16)
        y_ref[...] = _nn(hdn, wd_sc[...])

    @pl.when(tv_ref[t] == 0)
    def _():
        y_ref[...] = jnp.zeros_like(y_ref)


def _expert_ffn(tile_expert, tile_valid, token_of_row, xf3, w_gate, w_up, w_down):
    n_rows = token_of_row.shape[0]
    grid_spec = pltpu.PrefetchScalarGridSpec(
        num_scalar_prefetch=3,
        grid=(n_rows // MOE_TILE,),
        in_specs=[
            pl.BlockSpec(memory_space=pl.ANY),
            pl.BlockSpec((None, D_MODEL, D_FF), lambda t, te, tv, tok: (te[t], 0, 0)),
            pl.BlockSpec((None, D_MODEL, D_FF), lambda t, te, tv, tok: (te[t], 0, 0)),
            pl.BlockSpec((None, D_FF, D_MODEL), lambda t, te, tv, tok: (te[t], 0, 0)),
        ],
        out_specs=pl.BlockSpec((MOE_TILE, D_MODEL), lambda t, te, tv, tok: (t, 0)),
        scratch_shapes=[pltpu.VMEM((2, MOE_TILE * X_SLABS, LANES), F32),
                        pltpu.SemaphoreType.DMA((2,)),
                        pltpu.VMEM((D_MODEL, D_FF), BF16), pltpu.VMEM((D_MODEL, D_FF), BF16),
                        pltpu.VMEM((D_FF, D_MODEL), BF16)],
    )
    return pl.pallas_call(
        _expert_kernel,
        grid_spec=grid_spec,
        out_shape=jax.ShapeDtypeStruct((n_rows, D_MODEL), F32),
        compiler_params=_params("arbitrary"),
        name="expert_ffn",
    )(tile_expert, tile_valid, token_of_row, xf3, w_gate, w_up, w_down)


def _final_kernel(h2_ref, y0_ref, y1_ref, slab_ref, nw_ref, op_ref, os_ref, *, tiles_p):
    i = pl.program_id(0)
    slab = slab_ref[...]
    h = h2_ref[...] + (slab[:, 2:3] * y0_ref[...] + slab[:, 3:4] * y1_ref[...])
    y = _rms(h) * nw_ref[...]

    @pl.when(i < tiles_p)
    def _():
        op_ref[...] = y

    @pl.when(i >= tiles_p)
    def _():
        os_ref[...] = y


def _final(h2, y_pair, slab, norm_w, n_p):
    n = h2.shape[0]
    n_s = n - n_p
    tm = _pick_tile(n_s, 256)
    tiles_p, tiles_s = n_p // tm, n_s // tm
    return pl.pallas_call(
        functools.partial(_final_kernel, tiles_p=tiles_p),
        grid=(tiles_p + tiles_s,),
        in_specs=[
            pl.BlockSpec((tm, D_MODEL), lambda i: (i, 0)),
            pl.BlockSpec((None, tm, D_MODEL), lambda i: (0, i, 0)),
            pl.BlockSpec((None, tm, D_MODEL), lambda i: (1, i, 0)),
            pl.BlockSpec((tm, LANES), lambda i: (i, 0)),
            pl.BlockSpec((1, D_MODEL), lambda i: (0, 0)),
        ],
        out_specs=[pl.BlockSpec((tm, D_MODEL), lambda i: (jnp.minimum(i, tiles_p - 1), 0)),
                   pl.BlockSpec((tm, D_MODEL), lambda i: (jnp.maximum(i - tiles_p, 0), 0))],
        out_shape=[jax.ShapeDtypeStruct((n_p, D_MODEL), F32), jax.ShapeDtypeStruct((n_s, D_MODEL), F32)],
        compiler_params=_params("arbitrary"),
        name="combine_final_norm",
    )(h2, y_pair, y_pair, slab, norm_w)


def _rope_table(pos):
    half = ROPE // 2
    inv_freq = jnp.power(ROPE_THETA, -jnp.arange(half, dtype=F32) / half)
    ang = pos.astype(F32)[:, None] * inv_freq[None, :]
    cos, sin = jnp.cos(ang), jnp.sin(ang)
    return jnp.concatenate([cos, cos, sin, sin], axis=-1)


def _rotate_half_cols(w):
    half = ROPE // 2
    return jnp.concatenate([-w[..., half:], w[..., :half]], axis=-1)


def kernel(x_prompt, x_sample, cache_ckv, cache_krope, state_hgrn, page_table, meta_tokens, attn_norm_w, w_in, q_norm_w, w_uq, kv_norm_w, w_uk, w_uv, mla_out_norm_w, hgrn_lb_logits, hgrn_out_norm_w, w_out, ffn_norm_w, w_router_group, b_router_group, w_router_expert, b_router_expert, w_gate, w_up, w_down, final_norm_w):
    n_batch, seq, _ = x_prompt.shape
    n_dec, n_new, _ = x_sample.shape
    depth = w_in.shape[0]
    assert depth == 1
    n_p, n_s = n_batch * seq, n_dec * n_new
    past = page_table.shape[1] * cache_ckv.shape[2]

    wi = w_in[0]
    c0 = Q_LORA + KV_LORA
    w_kr = wi[:, c0:c0 + ROPE]
    w_z = jnp.concatenate([wi[:, :c0], wi[:, c0 + ROPE:], w_kr, _rotate_half_cols(w_kr),
                           jnp.zeros((D_MODEL, LANES), F32)], axis=1).astype(BF16)
    uq = w_uq[0]
    uq_rope = uq[:, :, HEAD_DIM:]
    w_q = jnp.concatenate([uq[:, :, :HEAD_DIM].reshape(Q_LORA, D_HALF),
                           uq_rope.reshape(Q_LORA, N_HEADS * ROPE),
                           _rotate_half_cols(uq_rope).reshape(Q_LORA, N_HEADS * ROPE)], axis=1).astype(BF16)
    w_ukt = jnp.transpose(w_uk[0], (1, 2, 0)).astype(BF16)
    w_uvt = jnp.transpose(w_uv[0], (1, 0, 2)).astype(BF16)
    w_o = w_out[0].astype(BF16)
    w_r = jnp.concatenate([w_router_group[0], w_router_expert[0],
                           jnp.zeros((D_MODEL, LANES - N_GROUPS - N_EXPERTS), F32)], axis=1)
    wr_hi = w_r.astype(BF16)
    wr_lo = (w_r - wr_hi.astype(F32)).astype(BF16)
    b_r = jnp.concatenate([b_router_group[0], b_router_expert[0],
                           jnp.zeros((LANES - N_GROUPS - N_EXPERTS,), F32)])[None, :]
    lb =jax.nn.softmax(hgrn_lb_logits.astype(F32), axis=0)[0][None, :]
    a_nw, q_nw, kv_nw = attn_norm_w[0][None, :], q_norm_w[0][None, :], kv_norm_w[0][None, :]
    mo_nw, ho_nw, f_nw = mla_out_norm_w[0][None, :], hgrn_out_norm_w[0][None, :], ffn_norm_w[0][None, :]

    z_m = _in_proj(meta_tokens, a_nw, w_z)
    _, _, ckv_m, kr_m, ckvb_m, krb_m = _qkv_prep(z_m, _rope_table(jnp.arange(N_META)), q_nw, kv_nw, w_q, w_ukt)
    _, s_meta = _hgrn(z_m[None], lb, ho_nw, jnp.zeros((1, N_HEADS, HEAD_DIM, HEAD_DIM), F32),
                      chunk=N_META, nbt=1, l_valid=N_META, state_kv=False)
    meta_c = jnp.pad(ckvb_m, ((0, LANES - N_META), (0, 0)))
    meta_r = jnp.pad(krb_m, ((0, LANES - N_META), (0, 0)))

    xp = x_prompt.reshape(n_p, D_MODEL)
    z_p = _in_proj(xp, a_nw, w_z)
    cs_p = jnp.tile(_rope_table(N_META + jnp.arange(seq)), (n_batch, 1))
    ql_p, qr_p, ckv_p, kr_p, ckvb_p, krb_p = _qkv_prep(z_p, cs_p, q_nw, kv_nw, w_q, w_ukt)
    def keys_with_meta(m, p, d):
        return jnp.concatenate([jnp.broadcast_to(m[None], (n_batch, LANES, d)), p.reshape(n_batch, seq, d)], axis=1)

    ya_p = _attn_prompt(ql_p, qr_p, keys_with_meta(meta_c, ckvb_p, KV_LORA),
                        keys_with_meta(meta_r, krb_p, ROPE), w_uvt, n_batch)
    s0_p = jnp.broadcast_to(s_meta, (n_batch, N_HEADS, HEAD_DIM, HEAD_DIM))
    yb_p, st_p = _hgrn(z_p.reshape(n_batch, seq, Z_WIDTH), lb, ho_nw, s0_p,
                       chunk=min(64, seq), nbt=n_batch, l_valid=seq, state_kv=False)
    st_p = jnp.swapaxes(st_p, -1, -2)

    xs = x_sample.reshape(n_s, D_MODEL)
    z_s = _in_proj(xs, a_nw, w_z)
    cs_s = jnp.tile(_rope_table(past + jnp.arange(n_new)), (n_dec, 1))
    ql_s, qr_s, ckv_s, kr_s, _, _ = _qkv_prep(z_s, cs_s, q_nw, kv_nw, w_q, w_ukt)

    def dec_rows(a):
        d = a.shape[-1]
        return jnp.transpose(a.reshape(N_HEADS, n_dec, n_new, d), (1, 0, 2, 3)).reshape(n_dec, N_HEADS * n_new, d)

    ya_s = _attn_decode(dec_rows(ql_s), dec_rows(qr_s), ckv_s.reshape(n_dec, n_new, KV_LORA),
                        kr_s.reshape(n_dec, n_new, ROPE), cache_ckv[0],
                        jnp.swapaxes(cache_krope[0], 1, 2), page_table, w_uvt)
    ya_s = jnp.transpose(ya_s.reshape(n_dec, N_HEADS, n_new, HEAD_DIM), (0, 2, 1, 3)).reshape(n_s, D_HALF)
    pad_new = -n_new % 8
    z_s3 = jnp.pad(z_s.reshape(n_dec, n_new, Z_WIDTH), ((0, 0), (0, pad_new), (0, 0)))
    yb_s, st_s = _hgrn(z_s3, lb, ho_nw, state_hgrn[0].astype(F32), chunk=n_new + pad_new,
                       nbt=math.gcd(n_dec, 4), l_valid=n_new, state_kv=True)
    yb_s = yb_s[:, :n_new].reshape(n_s, D_HALF)

    h2, xf, slab, cnt = _outproj_router(xp, xs, ya_p, ya_s, yb_p.reshape(n_p, D_HALF), yb_s,
                                        mo_nw, w_o, f_nw, wr_hi, wr_lo, b_r)

    n = n_p + n_s
    e_idx = slab[:, 0:2].astype(jnp.int32)
    rank = slab[:, 4:6].astype(jnp.int32)
    counts = cnt[0, N_GROUPS:N_GROUPS + N_EXPERTS].astype(jnp.int32)
    padded = (counts + MOE_TILE - 1) // MOE_TILE * MOE_TILE
    ends = jnp.cumsum(padded)
    pos = (ends - padded)[e_idx] + rank
    n_tiles = -(-2 * n // MOE_TILE) + N_EXPERTS
    starts = jnp.arange(n_tiles, dtype=jnp.int32) * MOE_TILE
    tile_valid = (starts < ends[-1]).astype(jnp.int32)
    last_valid = jnp.maximum(ends[-1] // MOE_TILE - 1, 0)
    tile_expert = jnp.sum((ends[None, :] <= jnp.minimum(starts, last_valid * MOE_TILE)[:, None])
                          .astype(jnp.int32), axis=1)
    token_of_row = jnp.zeros((n_tiles * MOE_TILE,), jnp.int32).at[pos.reshape(-1)].set(
        jnp.repeat(jnp.arange(n, dtype=jnp.int32), 2))
    y_sorted = _expert_ffn(tile_expert, tile_valid, token_of_row, xf.reshape(n, X_SLABS, LANES),
                           w_gate[0], w_up[0], w_down[0])
    y_slots = y_sorted.at[pos.T.reshape(-1)].get(mode="promise_in_bounds").reshape(2, n, D_MODEL)

    y_p, y_s = _final(h2, y_slots, slab, final_norm_w[None, :], n_p)

    def with_meta(m, p, d):
        return jnp.concatenate([jnp.broadcast_to(m[None], (n_batch, N_META, d)), p.reshape(n_batch, seq, d)], axis=1)[None]

    return (y_p.reshape(n_batch, seq, D_MODEL),
            y_s.reshape(n_dec, n_new, D_MODEL),
            with_meta(ckv_m, ckv_p, KV_LORA),
            with_meta(kr_m, kr_p, ROPE),
            st_p[None].astype(x_prompt.dtype),
            ckv_s.reshape(n_dec, n_new, KV_LORA)[None],
            kr_s.reshape(n_dec, n_new, ROPE)[None],
            st_s[None].astype(state_hgrn.dtype))
```

```python
import functools
import math

import jax
import jax.numpy as jnp
from jax import lax
from jax.experimental import pallas as pl
from jax.experimental.pallas import tpu as pltpu

F32 = jnp.float32
BF16 = jnp.bfloat16

D_MODEL = 2048
N_META = 16
N_HEADS = 8
HEAD_DIM = 128
ROPE = 64
Q_LORA = 512
KV_LORA = 512
D_HALF = N_HEADS * HEAD_DIM
ROPE_THETA = 10000.0
SM_SCALE = (HEAD_DIM + ROPE) ** -0.5
EPS = 1e-6
N_GROUPS = 4
EXP_PER_GROUP = 8
N_EXPERTS = N_GROUPS * EXP_PER_GROUP
D_FF = 512

Z_WIDTH = 5376
Z_TILE = 1792
Z_ROPE_BLOCK = 40

LANES = 128
VMEM_LIMIT = 56 * 1024 * 1024
NEG = -1e30
HGRN_SAFE_DECAY = 60.0
MOE_TILE = 256
X_SLABS = D_MODEL // LANES
GATHER_UNROLL = 8
PAGES_PER_CHUNK = 32
DECODE_SLOTS = 3


def _nn(a, b):
    return jnp.dot(a, b, preferred_element_type=F32)


def _nt(a, b):
    return lax.dot_general(a, b, (((1,), (1,)), ((), ())), preferred_element_type=F32)


def _tn(a, b):
    return lax.dot_general(a, b, (((0,), (0,)), ((), ())), preferred_element_type=F32)


def _split3(x):
    hi = x.astype(BF16)
    r1 = x - hi.astype(F32)
    mid = r1.astype(BF16)
    lo = (r1 - mid.astype(F32)).astype(BF16)
    return hi, mid, lo


def _rms(x):
    return x * lax.rsqrt(jnp.mean(x * x, axis=-1, keepdims=True) + EPS)


def _silu(x):
    return x * jax.nn.sigmoid(x)


def _pick_tile(n, pref):
    if n <= pref:
        return n
    for t in range(pref, 7, -1):
        if n % t == 0 and t % 8 == 0:
            return t
    raise ValueError(f"no tile for {n}")


def _params(*sem):
    return pltpu.CompilerParams(dimension_semantics=sem, vmem_limit_bytes=VMEM_LIMIT)


def _inproj_kernel(x_ref, nw_ref, w_ref, z_ref, xn_ref):
    @pl.when(pl.program_id(1) == 0)
    def _():
        xn_ref[...] = (_rms(x_ref[...]) * nw_ref[...]).astype(BF16)

    z_ref[...] = _nn(xn_ref[...], w_ref[...])


def _in_proj(x, norm_w, w_z):
    n = x.shape[0]
    tm = _pick_tile(n, 1024)
    return pl.pallas_call(
        _inproj_kernel,
        grid=(n // tm, Z_WIDTH // Z_TILE),
        in_specs=[
            pl.BlockSpec((tm, D_MODEL), lambda i, j: (i, 0)),
            pl.BlockSpec((1, D_MODEL), lambda i, j: (0, 0)),
            pl.BlockSpec((D_MODEL, Z_TILE), lambda i, j: (0, j)),
        ],
        out_specs=pl.BlockSpec((tm, Z_TILE), lambda i, j: (i, j)),
        out_shape=jax.ShapeDtypeStruct((n, Z_WIDTH), F32),
        scratch_shapes=[pltpu.VMEM((tm, D_MODEL), BF16)],
        compiler_params=_params("parallel", "arbitrary"),
        name="in_proj",
    )(x, norm_w, w_z)


def _prep_kernel(zq_ref, zkv_ref, zkr_ref, cs_ref, qnw_ref, kvnw_ref, wq_ref, wuk_ref,
                 ql_ref, qr_ref, ckv_ref, kr_ref, ckvb_ref, krb_ref):
    cqn = (_rms(zq_ref[...]) * qnw_ref[...]).astype(BF16)
    q = _nn(cqn, wq_ref[...])
    cs = cs_ref[...]
    cos = cs[:, :ROPE]
    sin = cs[:, ROPE:]
    for h in range(N_HEADS):
        qn = q[:, h * HEAD_DIM:(h + 1) * HEAD_DIM].astype(BF16)
        ql_ref[h] = (_nn(qn, wuk_ref[h]) * SM_SCALE).astype(BF16)
        a = q[:, D_HALF + h * ROPE:D_HALF + (h + 1) * ROPE]
        b = q[:, D_HALF + N_HEADS * ROPE + h * ROPE:D_HALF + N_HEADS * ROPE + (h + 1) * ROPE]
        qr_ref[h] = ((a * cos + b * sin) * SM_SCALE).astype(BF16)
    ckv = _rms(zkv_ref[...]) * kvnw_ref[...]
    ckv_ref[...] = ckv
    ckvb_ref[...] = ckv.astype(BF16)
    prod = zkr_ref[...] * cs
    kr = prod[:, :ROPE] + prod[:, ROPE:]
    kr_ref[...] = kr
    krb_ref[...] = kr.astype(BF16)


def _qkv_prep(z, cs, q_norm_w, kv_norm_w, w_q, w_ukt):
    n = z.shape[0]
    tm = _pick_tile(n, 256)
    const2 = lambda i: (0, 0)
    return pl.pallas_call(
        _prep_kernel,
        grid=(n // tm,),
        in_specs=[
            pl.BlockSpec((tm, Q_LORA), lambda i: (i, 0)),
            pl.BlockSpec((tm, KV_LORA), lambda i: (i, 1)),
            pl.BlockSpec((tm, LANES), lambda i: (i, Z_ROPE_BLOCK)),
            pl.BlockSpec((tm, LANES), lambda i: (i, 0)),
            pl.BlockSpec((1, Q_LORA), const2),
            pl.BlockSpec((1, KV_LORA), const2),
            pl.BlockSpec((Q_LORA, 2 * D_HALF), const2),
            pl.BlockSpec((N_HEADS, HEAD_DIM, KV_LORA), lambda i: (0, 0, 0)),
        ],
        out_specs=[
            pl.BlockSpec((N_HEADS, tm, KV_LORA), lambda i: (0, i, 0)),
            pl.BlockSpec((N_HEADS, tm, ROPE), lambda i: (0, i, 0)),
            pl.BlockSpec((tm, KV_LORA), lambda i: (i, 0)),
            pl.BlockSpec((tm, ROPE), lambda i: (i, 0)),
            pl.BlockSpec((tm, KV_LORA), lambda i: (i, 0)),
            pl.BlockSpec((tm, ROPE), lambda i: (i, 0)),
        ],
        out_shape=[
            jax.ShapeDtypeStruct((N_HEADS, n, KV_LORA), BF16),
            jax.ShapeDtypeStruct((N_HEADS, n, ROPE), BF16),
            jax.ShapeDtypeStruct((n, KV_LORA), F32),
            jax.ShapeDtypeStruct((n, ROPE), F32),
            jax.ShapeDtypeStruct((n, KV_LORA), BF16),
            jax.ShapeDtypeStruct((n, ROPE), BF16),
        ],
        compiler_params=_params("parallel"),
        name="qkv_prep",
    )(z, z, z, cs, q_norm_w, kv_norm_w, w_q, w_ukt)


def _attn_prompt_kernel(ql_ref, qr_ref, kc_ref, kr_ref, wuv_ref, o_ref, *, tq, tk, n_blocks):
    qi = pl.program_id(1)
    rows = N_HEADS * tq
    ql = ql_ref[...].reshape(rows, KV_LORA)
    qr = qr_ref[...].reshape(rows, ROPE)
    needed = (qi * tq + (tq - 1)) // tk + 1

    def prefix(nblk):
        width = LANES + nblk * tk
        kc = kc_ref[0:width, :]
        s = _nt(ql, kc) + _nt(qr, kr_ref[0:width, :])
        s_meta = s[:, :LANES]
        s_meta = jnp.where(lax.broadcasted_iota(jnp.int32, s_meta.shape, 1) < N_META, s_meta, NEG)
        s_last = s[:, width - tk:]
        row = lax.broadcasted_iota(jnp.int32, s_last.shape, 0)
        col = lax.broadcasted_iota(jnp.int32, s_last.shape, 1)
        s_last = jnp.where((nblk - 1) * tk + col <= qi * tq + (row & (tq - 1)), s_last, NEG)
        middle = [s[:, LANES:width - tk]] if nblk > 1 else []
        s = jnp.concatenate([s_meta] + middle + [s_last], axis=1)
        p = jnp.exp(s - jnp.max(s, axis=-1, keepdims=True))
        o = (_nn(p.astype(BF16), kc) / jnp.sum(p, axis=-1, keepdims=True)).astype(BF16)
        for h in range(N_HEADS):
            o_ref[:, h * HEAD_DIM:(h + 1) * HEAD_DIM] = _nn(o[h * tq:(h + 1) * tq], wuv_ref[h])

    for nblk in range(1, n_blocks + 1):
        pl.when(needed == nblk)(functools.partial(prefix, nblk))


def _attn_prompt(ql, qr, keys_c, keys_r, w_uvt, n_batch):
    seq = keys_c.shape[1] - LANES
    n = n_batch * seq
    tq = 128
    tk = min(512, seq)
    assert seq % tq == 0 and seq % tk == 0 and tq & (tq - 1) == 0
    qb = seq // tq
    return pl.pallas_call(
        functools.partial(_attn_prompt_kernel, tq=tq, tk=tk, n_blocks=seq // tk),
        grid=(n_batch, qb),
        in_specs=[
            pl.BlockSpec((N_HEADS, tq, KV_LORA), lambda b, i: (0, b * qb + i, 0)),
            pl.BlockSpec((N_HEADS, tq, ROPE), lambda b, i: (0, b * qb + i, 0)),
            pl.BlockSpec((None, LANES + seq, KV_LORA), lambda b, i: (b, 0, 0)),
            pl.BlockSpec((None, LANES + seq, ROPE), lambda b, i: (b, 0, 0)),
            pl.BlockSpec((N_HEADS, KV_LORA, HEAD_DIM), lambda b, i: (0, 0, 0)),
        ],
        out_specs=pl.BlockSpec((tq, D_HALF), lambda b, i: (b * qb + i, 0)),
        out_shape=jax.ShapeDtypeStruct((n, D_HALF), F32),
        compiler_params=_params("parallel", "parallel"),
        name="mla_prompt",
    )(ql, qr, keys_c, keys_r, w_uvt)


def _attn_decode_kernel(pt_ref, ql_ref, qr_ref, nc_ref, nr_ref, cc_hbm, cr_hbm, wuv_ref, o_ref,
                        cbuf, rbuf, sem, kc_sc, kr_sc, m_sc, l_sc, acc_sc,
                        *, chunk, n_chunks, n_batch, n_new, page):
    b = pl.program_id(0)
    total = n_batch * n_chunks
    ahead = DECODE_SLOTS - 1
    rows = N_HEADS * n_new
    ql = ql_ref[...]
    qr = qr_ref[...]

    def chunk_copies(g):
        bb = lax.div(g, jnp.int32(n_chunks))
        first_page = (g - bb * n_chunks) * chunk
        slot = lax.rem(g, jnp.int32(DECODE_SLOTS))
        copies = []
        for k in range(chunk):
            pg = pt_ref[bb, first_page + k]
            copies.append(pltpu.make_async_copy(cc_hbm.at[pg], cbuf.at[slot, k], sem.at[slot]))
            copies.append(pltpu.make_async_copy(cr_hbm.at[pg], rbuf.at[slot, k], sem.at[slot]))
        return copies

    @pl.when(b == 0)
    def _():
        for g in range(min(ahead, total)):
            for cp in chunk_copies(jnp.int32(g)):
                cp.start()

    qlf = ql.astype(F32)
    qrf = qr.astype(F32)
    nc = nc_ref[...]
    nr = nr_ref[...]
    tok = lax.rem(lax.broadcasted_iota(jnp.int32, (rows, 1), 0), n_new)
    cols = []
    for t in range(n_new):
        sc = (jnp.sum(qlf * nc[t:t + 1, :], axis=-1, keepdims=True)
              + jnp.sum(qrf * nr[t:t + 1, :], axis=-1, keepdims=True))
        cols.append(jnp.where(tok >= t, sc, NEG))
    m0 = functools.reduce(jnp.maximum, cols)
    ps = [jnp.exp(c - m0) for c in cols]
    m_sc[...] = m0
    l_sc[...] = functools.reduce(jnp.add, ps)
    acc_sc[...] = functools.reduce(jnp.add, [p * nc[t:t + 1, :] for t, p in enumerate(ps)])

    def chunk_body(c, carry):
        g = b * n_chunks + c
        slot = lax.rem(g, jnp.int32(DECODE_SLOTS))
        for cp in chunk_copies(g):
            cp.wait()

        @pl.when(g + ahead < total)
        def _():
            for cp in chunk_copies(g + ahead):
                cp.start()

        for k in range(chunk):
            kc_sc[k * page:(k + 1) * page, :] = cbuf[slot, k].astype(BF16)
            kr_sc[:, k * page:(k + 1) * page] = rbuf[slot, k].astype(BF16)
        kc = kc_sc[...]
        s = _nt(ql, kc) + _nn(qr, kr_sc[...])
        m_prev = m_sc[...]
        m_new = jnp.maximum(m_prev, jnp.max(s, axis=-1, keepdims=True))
        alpha = jnp.exp(m_prev - m_new)
        p = jnp.exp(s - m_new)
        l_sc[...] = alpha * l_sc[...] + jnp.sum(p, axis=-1, keepdims=True)
        acc_sc[...] = alpha * acc_sc[...] + _nn(p.astype(BF16), kc)
        m_sc[...] = m_new
        return carry

    lax.fori_loop(0, n_chunks, chunk_body, 0)

    o = (acc_sc[...] / l_sc[...]).astype(BF16)
    head = lax.broadcasted_iota(jnp.int32, (rows, 1), 0) // n_new
    y = jnp.zeros((rows, HEAD_DIM), F32)
    for h in range(N_HEADS):
        y = jnp.where(head == h, _nn(o, wuv_ref[h]), y)
    o_ref[...] = y


def _attn_decode(ql, qr, new_c, new_r, cache_c, cache_r, page_table, w_uvt):
    nb, rows, _ = ql.shape
    n_new = rows // N_HEADS
    total_pages = page_table.shape[1]
    chunk = min(PAGES_PER_CHUNK, total_pages)
    n_chunks = total_pages // chunk
    assert total_pages % chunk == 0
    page = cache_c.shape[1]

    row3 = lambda b, pt: (b, 0, 0)
    grid_spec = pltpu.PrefetchScalarGridSpec(
        num_scalar_prefetch=1,
        grid=(nb,),
        in_specs=[
            pl.BlockSpec((None, rows, KV_LORA), row3),
            pl.BlockSpec((None, rows, ROPE), row3),
            pl.BlockSpec((None, n_new, KV_LORA), row3),
            pl.BlockSpec((None, n_new, ROPE), row3),
            pl.BlockSpec(memory_space=pl.ANY),
            pl.BlockSpec(memory_space=pl.ANY),
            pl.BlockSpec((N_HEADS, KV_LORA, HEAD_DIM), lambda b, pt: (0, 0, 0)),
        ],
        out_specs=pl.BlockSpec((None, rows, HEAD_DIM), row3),
        scratch_shapes=[pltpu.VMEM((DECODE_SLOTS, chunk, page, KV_LORA), F32),
                        pltpu.VMEM((DECODE_SLOTS, chunk, ROPE, page), F32),
                        pltpu.SemaphoreType.DMA((DECODE_SLOTS,)),
                        pltpu.VMEM((chunk * page, KV_LORA), BF16),
                        pltpu.VMEM((ROPE, chunk * page), BF16),
                        pltpu.VMEM((rows, 1), F32), pltpu.VMEM((rows, 1), F32),
                        pltpu.VMEM((rows, KV_LORA), F32)],
    )
    return pl.pallas_call(
        functools.partial(_attn_decode_kernel, chunk=chunk, n_chunks=n_chunks, n_batch=nb, n_new=n_new,
                          page=page),
        grid_spec=grid_spec,
        out_shape=jax.ShapeDtypeStruct((nb, rows, HEAD_DIM), F32),
        compiler_params=_params("arbitrary"),
        name="mla_decode",
    )(page_table, ql, qr, new_c, new_r, cache_c, cache_r, w_uvt)


def _hgrn_gates(qb, fb, ib, gb, lb, keep):
    logf = jnp.log(lb + (1.0 - lb) * jax.nn.sigmoid(fb))
    kk = (1.0 - lb) * jax.nn.sigmoid(-fb)
    if keep is not None:
        logf = jnp.where(keep, logf, 0.0)
        kk = jnp.where(keep, kk, 0.0)
    return logf, kk, _silu(qb), ib, _silu(gb)


def _hgrn_kernel(qb_ref, fb_ref, ib_ref, gb_ref, lb_ref, nw_ref, s0_ref, y_ref, s_ref, st_sc,
                 *, nbt, chunk, l_valid, l_total, state_kv):
    c = pl.program_id(1)

    @pl.when(c == 0)
    def _():
        for nb in range(nbt):
            for h in range(N_HEADS):
                st_sc[nb, h] = s0_ref[nb, h].T if state_kv else s0_ref[nb, h]

    lb = lb_ref[...]
    nw = nw_ref[...]
    t_row = lax.broadcasted_iota(jnp.int32, (chunk, chunk), 0)
    t_col = lax.broadcasted_iota(jnp.int32, (chunk, chunk), 1)
    causal = t_row >= t_col
    tri = jnp.where(causal, 1.0, 0.0).astype(BF16)
    tok = lax.broadcasted_iota(jnp.int32, (chunk, 1), 0)
    keep = (c * chunk + tok < l_valid) if l_valid < l_total else None

    gates = [_hgrn_gates(qb_ref[nb], fb_ref[nb], ib_ref[nb], gb_ref[nb], lb, keep) for nb in range(nbt)]
    decay = functools.reduce(jnp.maximum, [-jnp.sum(g[0], axis=0, keepdims=True) for g in gates])
    safe = jnp.max(decay) < HGRN_SAFE_DECAY

    def heads(h):
        return slice(h * HEAD_DIM, (h + 1) * HEAD_DIM)

    @pl.when(safe)
    def _():
        for nb in range(nbt):
            logf, kk, q, v, gate = gates[nb]
            hi, mid, lo = _split3(logf)
            b = _nn(tri, hi) + _nn(tri, mid) + _nn(tri, lo)
            b_end = b[chunk - 1:chunk, :]
            qd = (q * jnp.exp(b)).astype(BF16)
            ke = (kk * jnp.exp(-b)).astype(BF16)
            wd = (kk * jnp.exp(b_end - b)).astype(BF16)
            e_end = jnp.exp(b_end)
            vb = v.astype(BF16)
            for h in range(N_HEADS):
                hs = heads(h)
                a = jnp.where(causal, _nt(qd[:, hs], ke[:, hs]), 0.0).astype(BF16)
                st = st_sc[nb, h]
                o = _nn(a, vb[:, hs]) + _nt(qd[:, hs], st.astype(BF16))
                st_sc[nb, h] = st * e_end[:, hs] + _tn(vb[:, hs], wd[:, hs])
                y_ref[nb, :, hs] = (_rms(o) * nw[:, hs] * gate[:, hs]).astype(BF16)

    @pl.when(jnp.logical_not(safe))
    def _():
        def batch_body(nb, carry):
            logf, kk, q, v, gate = _hgrn_gates(qb_ref[nb], fb_ref[nb], ib_ref[nb], gb_ref[nb], lb, keep)
            for h in range(N_HEADS):
                hs = heads(h)
                lf, qh, vh = logf[:, hs], q[:, hs], v[:, hs]
                kh = kk[:, hs].astype(BF16)

                def token(t, carry_t):
                    st, o = carry_t
                    here = tok == t
                    f_row = jnp.exp(jnp.sum(jnp.where(here, lf, 0.0), axis=0, keepdims=True))
                    st = st * f_row + _tn(jnp.where(here, vh, 0.0).astype(BF16), kh)
                    o = o + _nt(jnp.where(here, qh, 0.0).astype(BF16), st.astype(BF16))
                    return st, o

                st, o = lax.fori_loop(0, chunk, token,
                                      (st_sc[nb, h], jnp.zeros((chunk, HEAD_DIM), F32)))
                st_sc[nb, h] = st
                y_ref[nb, :, hs] = (_rms(o) * nw[:, hs] * gate[:, hs]).astype(BF16)
            return carry

        lax.fori_loop(0, nbt, batch_body, 0)

    @pl.when(c == pl.num_programs(1) - 1)
    def _():
        for nb in range(nbt):
            for h in range(N_HEADS):
                s_ref[nb, h] = st_sc[nb, h].T if state_kv else st_sc[nb, h]


def _hgrn(z3, lb, norm_w, s0, *, chunk, nbt, l_valid, state_kv):
    nb, l_total, _ = z3.shape
    assert l_total % chunk == 0 and nb % nbt == 0

    def zspec(k):
        return pl.BlockSpec((nbt, chunk, D_HALF), lambda b, c: (b, c, k))

    sshape = (nbt, N_HEADS, HEAD_DIM, HEAD_DIM)
    sspec = pl.BlockSpec(sshape, lambda b, c: (b, 0, 0, 0))
    vec = pl.BlockSpec((1, D_HALF), lambda b, c: (0, 0))
    return pl.pallas_call(
        functools.partial(_hgrn_kernel, nbt=nbt, chunk=chunk, l_valid=l_valid, l_total=l_total,
                          state_kv=state_kv),
        grid=(nb // nbt, l_total // chunk),
        in_specs=[zspec(1), zspec(2), zspec(3), zspec(4), vec, vec, sspec],
        out_specs=[pl.BlockSpec((nbt, chunk, D_HALF), lambda b, c: (b, c, 0)), sspec],
        out_shape=[jax.ShapeDtypeStruct((nb, l_total, D_HALF), BF16),
                   jax.ShapeDtypeStruct(s0.shape, F32)],
        scratch_shapes=[pltpu.VMEM(sshape, F32)],
        compiler_params=_params("parallel", "arbitrary"),
        name="hgrn",
    )(z3, z3, z3, z3, lb, norm_w, s0)


def _outproj_router_kernel(hp_ref, hs_ref, yap_ref, yas_ref, ybp_ref, ybs_ref, anw_ref, wout_ref,
                           fnw_ref, wrh_ref, wrl_ref, br_ref,
                           h2_ref, xf_ref, slab_ref, slabt_ref, cnt_ref, *, tiles_p, tm):
    i = pl.program_id(0)
    is_p = i < tiles_p
    h = jnp.where(is_p, hp_ref[...], hs_ref[...])
    ya = jnp.where(is_p, yap_ref[...], yas_ref[...])
    yb = jnp.where(is_p, ybp_ref[...], ybs_ref[...])
    ya = (_rms(ya) * anw_ref[...]).astype(BF16)
    h2 = h + (_nn(ya, wout_ref[:D_HALF, :]) + _nn(yb, wout_ref[D_HALF:, :]))
    h2_ref[...] = h2
    xf = _rms(h2) * fnw_ref[...]
    xf_ref[...] = xf
    x_hi = xf.astype(BF16)
    x_lo = (xf - x_hi.astype(F32)).astype(BF16)
    w_hi = wrh_ref[...]
    logits = _nn(x_hi, w_hi) + _nn(x_lo, w_hi) + _nn(x_hi, wrl_ref[...]) + br_ref[...]
    lane = lax.broadcasted_iota(jnp.int32, logits.shape, 1)

    def first_max(x):
        m = jnp.max(x, axis=-1, keepdims=True)
        return m, jnp.min(jnp.where(x == m, lane, LANES), axis=-1, keepdims=True)

    gl = jnp.where(lane < N_GROUPS, logits, NEG)
    g_max, g_idx = first_max(gl)
    pg_top = 1.0 / jnp.sum(jnp.exp(gl - g_max), axis=-1, keepdims=True)
    lo = N_GROUPS + EXP_PER_GROUP * g_idx
    el = jnp.where((lane >= lo) & (lane < lo + EXP_PER_GROUP), logits, NEG)
    e1, i1 = first_max(el)
    e2, i2 = first_max(jnp.where(lane == i1, NEG, el))
    r = jnp.exp(e2 - e1)
    w1 = pg_top / (1.0 + r)
    w2 = w1 * r

    @pl.when(i == 0)
    def _():
        cnt_ref[...] = jnp.zeros_like(cnt_ref)

    onehot = jnp.where((lane == i1) | (lane == i2), 1.0, 0.0)
    t_row = lax.broadcasted_iota(jnp.int32, (tm, tm), 0)
    t_col = lax.broadcasted_iota(jnp.int32, (tm, tm), 1)
    before = jnp.where(t_row > t_col, 1.0, 0.0).astype(BF16)
    prefix = _nn(before, onehot.astype(BF16)) + cnt_ref[0:1, :]
    r1 = jnp.sum(jnp.where(lane == i1, prefix, 0.0), axis=-1, keepdims=True)
    r2 = jnp.sum(jnp.where(lane == i2, prefix, 0.0), axis=-1, keepdims=True)
    cnt_ref[...] = cnt_ref[...] + jnp.sum(onehot, axis=0, keepdims=True)
    cols = [(i1 - N_GROUPS).astype(F32), (i2 - N_GROUPS).astype(F32), w1, w2, r1, r2]
    slab = jnp.zeros(logits.shape, F32)
    for k, val in enumerate(cols):
        slab = jnp.where(lane == k, val, slab)
    slab_ref[...] = slab
    slabt_ref[...] = slab.T[:8, :]


def _outproj_router(h_p, h_s, ya_p, ya_s, yb_p, yb_s, a_norm_w, w_out, f_norm_w, wr_hi, wr_lo, b_r):
    n_p, n_s = h_p.shape[0], h_s.shape[0]
    tm = _pick_tile(n_s, 256)
    assert n_p % tm == 0
    tiles_p, tiles_s = n_p // tm, n_s // tm
    n = n_p + n_s
    pmap = lambda i: (jnp.minimum(i, tiles_p - 1), 0)
    smap = lambda i: (jnp.maximum(i - tiles_p, 0), 0)
    const = lambda i: (0, 0)
    row = lambda i: (i, 0)
    return pl.pallas_call(
        functools.partial(_outproj_router_kernel, tiles_p=tiles_p, tm=tm),
        grid=(tiles_p + tiles_s,),
        in_specs=[
            pl.BlockSpec((tm, D_MODEL), pmap), pl.BlockSpec((tm, D_MODEL), smap),
            pl.BlockSpec((tm, D_HALF), pmap), pl.BlockSpec((tm, D_HALF), smap),
            pl.BlockSpec((tm, D_HALF), pmap), pl.BlockSpec((tm, D_HALF), smap),
            pl.BlockSpec((1, D_HALF), const),
            pl.BlockSpec((D_MODEL, D_MODEL), const),
            pl.BlockSpec((1, D_MODEL), const),
            pl.BlockSpec((D_MODEL, LANES), const), pl.BlockSpec((D_MODEL, LANES), const),
            pl.BlockSpec((1, LANES), const),
        ],
        out_specs=[pl.BlockSpec((tm, D_MODEL), row), pl.BlockSpec((tm, D_MODEL), row),
                   pl.BlockSpec((tm, LANES), row), pl.BlockSpec((8, tm), lambda i: (0, i)),
                   pl.BlockSpec((8, LANES), const)],
        out_shape=[jax.ShapeDtypeStruct((n, D_MODEL), F32), jax.ShapeDtypeStruct((n, D_MODEL), F32),
                   jax.ShapeDtypeStruct((n, LANES), F32), jax.ShapeDtypeStruct((8, n), F32),
                   jax.ShapeDtypeStruct((8, LANES), F32)],
        compiler_params=_params("arbitrary"),
        name="outproj_router",
    )(h_p, h_s, ya_p, ya_s, yb_p, yb_s, a_norm_w, w_out, f_norm_w, wr_hi, wr_lo, b_r)


def _expert_kernel(te_ref, tv_ref, tr_ref, tok_ref, x_hbm, wg_ref, wu_ref, wd_ref, y_ref,
                   xbuf, sem, wg_sc, wu_sc, wd_sc):
    t = pl.program_id(0)
    n_tiles = pl.num_programs(0)
    slot = lax.rem(t, 2)

    def row_copy(tile, r, s):
        tok = tok_ref[tile * MOE_TILE + r]
        return pltpu.make_async_copy(x_hbm.at[tok], xbuf.at[s, pl.ds(r * X_SLABS, X_SLABS)], sem.at[s])

    def row_groups(tile):
        return (tr_ref[tile] + (GATHER_UNROLL - 1)) // GATHER_UNROLL

    def start_tile(tile, s):
        def body(i, carry):
            for j in range(GATHER_UNROLL):
                row_copy(tile, i * GATHER_UNROLL + j, s).start()
            return carry
        lax.fori_loop(0, row_groups(tile), body, 0)

    @pl.when(t == 0)
    def _():
        xbuf[...] = jnp.zeros_like(xbuf)
        start_tile(0, 0)

    nxt = jnp.minimum(t + 1, n_tiles - 1)

    @pl.when((t + 1 < n_tiles) & (tv_ref[nxt] == 1))
    def _():
        start_tile(nxt, 1 - slot)

    @pl.when((t == 0) | (te_ref[t] != te_ref[jnp.maximum(t - 1, 0)]))
    def _():
        wg_sc[...] = wg_ref[...].astype(BF16)
        wu_sc[...] = wu_ref[...].astype(BF16)
        wd_sc[...] = wd_ref[...].astype(BF16)

    @pl.when(tv_ref[t] == 1)
    def _():
        def body(i, carry):
            for j in range(GATHER_UNROLL):
                row_copy(t, i * GATHER_UNROLL + j, slot).wait()
            return carry
        lax.fori_loop(0, row_groups(t), body, 0)
        x = jnp.concatenate([xbuf[slot, pl.ds(s, MOE_TILE, stride=X_SLABS), :] for s in range(X_SLABS)],
                            axis=1).astype(BF16)
        hdn = (_silu(_nn(x, wg_sc[...])) * _nn(x, wu_sc[...])).astype(BF16)
        y_ref[...] = _nn(hdn, wd_sc[...])

    @pl.when(tv_ref[t] == 0)
    def _():
        y_ref[...] = jnp.zeros_like(y_ref)


def _expert_ffn(tile_expert, tile_valid, tile_rows, token_of_row, xf3, w_gate, w_up, w_down):
    n_rows = token_of_row.shape[0]
    grid_spec = pltpu.PrefetchScalarGridSpec(
        num_scalar_prefetch=4,
        grid=(n_rows // MOE_TILE,),
        in_specs=[
            pl.BlockSpec(memory_space=pl.ANY),
            pl.BlockSpec((None, D_MODEL, D_FF), lambda t, te, tv, tr, tok: (te[t], 0, 0)),
            pl.BlockSpec((None, D_MODEL, D_FF), lambda t, te, tv, tr, tok: (te[t], 0, 0)),
            pl.BlockSpec((None, D_FF, D_MODEL), lambda t, te, tv, tr, tok: (te[t], 0, 0)),
        ],
        out_specs=pl.BlockSpec((MOE_TILE, D_MODEL), lambda t, te, tv, tr, tok: (t, 0)),
        scratch_shapes=[pltpu.VMEM((2, MOE_TILE * X_SLABS, LANES), F32),
                        pltpu.SemaphoreType.DMA((2,)),
                        pltpu.VMEM((D_MODEL, D_FF), BF16), pltpu.VMEM((D_MODEL, D_FF), BF16),
                        pltpu.VMEM((D_FF, D_MODEL), BF16)],
    )
    return pl.pallas_call(
        _expert_kernel,
        grid_spec=grid_spec,
        out_shape=jax.ShapeDtypeStruct((n_rows, D_MODEL), F32),
        compiler_params=_params("arbitrary"),
        name="expert_ffn",
    )(tile_expert, tile_valid, tile_rows, token_of_row, xf3, w_gate, w_up, w_down)


def _final_kernel(h2_ref, y0_ref, y1_ref, slab_ref, nw_ref, op_ref, os_ref, *, tiles_p):
    i = pl.program_id(0)
    slab = slab_ref[...]
    h = h2_ref[...] + (slab[:, 2:3] * y0_ref[...] + slab[:, 3:4] * y1_ref[...])
    y = _rms(h) * nw_ref[...]

    @pl.when(i < tiles_p)
    def _():
        op_ref[...] = y

    @pl.when(i >= tiles_p)
    def _():
        os_ref[...] = y


def _final(h2, y_pair, slab, norm_w, n_p):
    n = h2.shape[0]
    n_s = n - n_p
    tm = _pick_tile(n_s, 256)
    tiles_p, tiles_s = n_p // tm, n_s // tm
    return pl.pallas_call(
        functools.partial(_final_kernel, tiles_p=tiles_p),
        grid=(tiles_p + tiles_s,),
        in_specs=[
            pl.BlockSpec((tm, D_MODEL), lambda i: (i, 0)),
            pl.BlockSpec((None, tm, D_MODEL), lambda i: (0, i, 0)),
            pl.BlockSpec((None, tm, D_MODEL), lambda i: (1, i, 0)),
            pl.BlockSpec((tm, LANES), lambda i: (i, 0)),
            pl.BlockSpec((1, D_MODEL), lambda i: (0, 0)),
        ],
        out_specs=[pl.BlockSpec((tm, D_MODEL), lambda i: (jnp.minimum(i, tiles_p - 1), 0)),
                   pl.BlockSpec((tm, D_MODEL), lambda i: (jnp.maximum(i - tiles_p, 0), 0))],
        out_shape=[jax.ShapeDtypeStruct((n_p, D_MODEL), F32), jax.ShapeDtypeStruct((n_s, D_MODEL), F32)],
        compiler_params=_params("arbitrary"),
        name="combine_final_norm",
    )(h2, y_pair, y_pair, slab, norm_w)


def _rope_table(pos):
    half = ROPE // 2
    inv_freq = jnp.power(ROPE_THETA, -jnp.arange(half, dtype=F32) / half)
    ang = pos.astype(F32)[:, None] * inv_freq[None, :]
    cos, sin = jnp.cos(ang), jnp.sin(ang)
    return jnp.concatenate([cos, cos, sin, sin], axis=-1)


def _rotate_half_cols(w):
    half = ROPE // 2
    return jnp.concatenate([-w[..., half:], w[..., :half]], axis=-1)


def kernel(x_prompt, x_sample, cache_ckv, cache_krope, state_hgrn, page_table, meta_tokens, attn_norm_w, w_in, q_norm_w, w_uq, kv_norm_w, w_uk, w_uv, mla_out_norm_w, hgrn_lb_logits, hgrn_out_norm_w, w_out, ffn_norm_w, w_router_group, b_router_group, w_router_expert, b_router_expert, w_gate, w_up, w_down, final_norm_w):
    n_batch, seq, _ = x_prompt.shape
    n_dec, n_new, _ = x_sample.shape
    depth = w_in.shape[0]
    assert depth == 1
    n_p, n_s = n_batch * seq, n_dec * n_new
    past = page_table.shape[1] * cache_ckv.shape[2]

    wi = w_in[0]
    c0 = Q_LORA + KV_LORA
    w_kr = wi[:, c0:c0 + ROPE]
    w_z = jnp.concatenate([wi[:, :c0], wi[:, c0 + ROPE:], w_kr, _rotate_half_cols(w_kr),
                           jnp.zeros((D_MODEL, LANES), F32)], axis=1).astype(BF16)
    uq = w_uq[0]
    uq_rope = uq[:, :, HEAD_DIM:]
    w_q = jnp.concatenate([uq[:, :, :HEAD_DIM].reshape(Q_LORA, D_HALF),
                           uq_rope.reshape(Q_LORA, N_HEADS * ROPE),
                           _rotate_half_cols(uq_rope).reshape(Q_LORA, N_HEADS * ROPE)], axis=1).astype(BF16)
    w_ukt = jnp.transpose(w_uk[0], (1, 2, 0)).astype(BF16)
    w_uvt = jnp.transpose(w_uv[0], (1, 0, 2)).astype(BF16)
    w_o = w_out[0].astype(BF16)
    w_r = jnp.concatenate([w_router_group[0], w_router_expert[0],
                           jnp.zeros((D_MODEL, LANES - N_GROUPS - N_EXPERTS), F32)], axis=1)
    wr_hi = w_r.astype(BF16)
    wr_lo = (w_r - wr_hi.astype(F32)).astype(BF16)
    b_r = jnp.concatenate([b_router_group[0], b_router_expert[0],
                           jnp.zeros((LANES - N_GROUPS - N_EXPERTS,), F32)])[None, :]
    lb =jax.nn.softmax(hgrn_lb_logits.astype(F32), axis=0)[0][None, :]
    a_nw, q_nw, kv_nw = attn_norm_w[0][None, :], q_norm_w[0][None, :], kv_norm_w[0][None, :]
    mo_nw, ho_nw, f_nw = mla_out_norm_w[0][None, :], hgrn_out_norm_w[0][None, :], ffn_norm_w[0][None, :]

    z_m = _in_proj(meta_tokens, a_nw, w_z)
    _, _, ckv_m, kr_m, ckvb_m, krb_m = _qkv_prep(z_m, _rope_table(jnp.arange(N_META)), q_nw, kv_nw, w_q, w_ukt)
    _, s_meta = _hgrn(z_m[None], lb, ho_nw, jnp.zeros((1, N_HEADS, HEAD_DIM, HEAD_DIM), F32),
                      chunk=N_META, nbt=1, l_valid=N_META, state_kv=False)
    meta_c = jnp.pad(ckvb_m, ((0, LANES - N_META), (0, 0)))
    meta_r = jnp.pad(krb_m, ((0, LANES - N_META), (0, 0)))

    xp = x_prompt.reshape(n_p, D_MODEL)
    z_p = _in_proj(xp, a_nw, w_z)
    cs_p = jnp.tile(_rope_table(N_META + jnp.arange(seq)), (n_batch, 1))
    ql_p, qr_p, ckv_p, kr_p, ckvb_p, krb_p = _qkv_prep(z_p, cs_p, q_nw, kv_nw, w_q, w_ukt)
    def keys_with_meta(m, p, d):
        return jnp.concatenate([jnp.broadcast_to(m[None], (n_batch, LANES, d)), p.reshape(n_batch, seq, d)], axis=1)

    ya_p = _attn_prompt(ql_p, qr_p, keys_with_meta(meta_c, ckvb_p, KV_LORA),
                        keys_with_meta(meta_r, krb_p, ROPE), w_uvt, n_batch)
    s0_p = jnp.broadcast_to(s_meta, (n_batch, N_HEADS, HEAD_DIM, HEAD_DIM))
    yb_p, st_p = _hgrn(z_p.reshape(n_batch, seq, Z_WIDTH), lb, ho_nw, s0_p,
                       chunk=min(64, seq), nbt=n_batch, l_valid=seq, state_kv=False)
    st_p = jnp.swapaxes(st_p, -1, -2)

    xs = x_sample.reshape(n_s, D_MODEL)
    z_s = _in_proj(xs, a_nw, w_z)
    cs_s = jnp.tile(_rope_table(past + jnp.arange(n_new)), (n_dec, 1))
    ql_s, qr_s, ckv_s, kr_s, _, _ = _qkv_prep(z_s, cs_s, q_nw, kv_nw, w_q, w_ukt)

    def dec_rows(a):
        d = a.shape[-1]
        return jnp.transpose(a.reshape(N_HEADS, n_dec, n_new, d), (1, 0, 2, 3)).reshape(n_dec, N_HEADS * n_new, d)

    ya_s = _attn_decode(dec_rows(ql_s), dec_rows(qr_s), ckv_s.reshape(n_dec, n_new, KV_LORA),
                        kr_s.reshape(n_dec, n_new, ROPE), cache_ckv[0],
                        jnp.swapaxes(cache_krope[0], 1, 2), page_table, w_uvt)
    ya_s = jnp.transpose(ya_s.reshape(n_dec, N_HEADS, n_new, HEAD_DIM), (0, 2, 1, 3)).reshape(n_s, D_HALF)
    pad_new = -n_new % 8
    z_s3 = jnp.pad(z_s.reshape(n_dec, n_new, Z_WIDTH), ((0, 0), (0, pad_new), (0, 0)))
    yb_s, st_s = _hgrn(z_s3, lb, ho_nw, state_hgrn[0].astype(F32), chunk=n_new + pad_new,
                       nbt=math.gcd(n_dec, 4), l_valid=n_new, state_kv=True)
    yb_s = yb_s[:, :n_new].reshape(n_s, D_HALF)

    h2, xf, slab, slab_t, cnt = _outproj_router(xp, xs, ya_p, ya_s, yb_p.reshape(n_p, D_HALF), yb_s,
                                        mo_nw, w_o, f_nw, wr_hi, wr_lo, b_r)

    n = n_p + n_s
    routed = slab_t.astype(jnp.int32)
    counts = cnt[0, N_GROUPS:N_GROUPS + N_EXPERTS].astype(jnp.int32)
    padded = (counts + MOE_TILE - 1) // MOE_TILE * MOE_TILE
    ends = jnp.cumsum(padded)
    first = ends - padded
    pos = jnp.concatenate([first[routed[0]] + routed[4], first[routed[1]] + routed[5]])
    n_tiles = -(-2 * n // MOE_TILE) + N_EXPERTS
    starts = jnp.arange(n_tiles, dtype=jnp.int32) * MOE_TILE
    tile_valid = (starts < ends[-1]).astype(jnp.int32)
    last_valid = jnp.maximum(ends[-1] // MOE_TILE - 1, 0)
    tile_expert = jnp.sum((ends[None, :] <= jnp.minimum(starts, last_valid * MOE_TILE)[:, None])
                          .astype(jnp.int32), axis=1)
    tile_rows = jnp.clip(counts[tile_expert] - (starts - first[tile_expert]), 0, MOE_TILE) * tile_valid
    tokens = jnp.arange(n, dtype=jnp.int32)
    token_of_row = jnp.zeros((n_tiles * MOE_TILE,), jnp.int32).at[pos].set(jnp.concatenate([tokens, tokens]))
    y_sorted = _expert_ffn(tile_expert, tile_valid, tile_rows, token_of_row, xf.reshape(n, X_SLABS, LANES),
                           w_gate[0], w_up[0], w_down[0])
    y_slots = y_sorted.at[pos].get(mode="promise_in_bounds").reshape(2, n, D_MODEL)

    y_p, y_s = _final(h2, y_slots, slab, final_norm_w[None, :], n_p)

    def with_meta(m, p, d):
        return jnp.concatenate([jnp.broadcast_to(m[None], (n_batch, N_META, d)), p.reshape(n_batch, seq, d)], axis=1)[None]

    return (y_p.reshape(n_batch, seq, D_MODEL),
            y_s.reshape(n_dec, n_new, D_MODEL),
            with_meta(ckv_m, ckv_p, KV_LORA),
            with_meta(kr_m, kr_p, ROPE),
            st_p[None].astype(x_prompt.dtype),
            ckv_s.reshape(n_dec, n_new, KV_LORA)[None],
            kr_s.reshape(n_dec, n_new, ROPE)[None],
            st_s[None].astype(state_hgrn.dtype))
```

```python
import functools
import math

import jax
import jax.numpy as jnp
from jax import lax
from jax.experimental import pallas as pl
from jax.experimental.pallas import tpu as pltpu

F32 = jnp.float32
BF16 = jnp.bfloat16

D_MODEL = 2048
N_META = 16
N_HEADS = 8
HEAD_DIM = 128
ROPE = 64
Q_LORA = 512
KV_LORA = 512
D_HALF = N_HEADS * HEAD_DIM
ROPE_THETA = 10000.0
SM_SCALE = (HEAD_DIM + ROPE) ** -0.5
EPS = 1e-6
N_GROUPS = 4
EXP_PER_GROUP = 8
N_EXPERTS = N_GROUPS * EXP_PER_GROUP
D_FF = 512

Z_WIDTH = 5376
Z_TILE = 1792
Z_ROPE_BLOCK = 40

LANES = 128
VMEM_LIMIT = 56 * 1024 * 1024
NEG = -1e30
HGRN_SAFE_DECAY = 60.0
HGRN_CHUNK = 128
MOE_TILE = 256
X_SLABS = D_MODEL // LANES
GATHER_UNROLL = 8
PAGES_PER_CHUNK = 32
DECODE_SLOTS = 3


def _nn(a, b):
    return jnp.dot(a, b, preferred_element_type=F32)


def _nt(a, b):
    return lax.dot_general(a, b, (((1,), (1,)), ((), ())), preferred_element_type=F32)


def _tn(a, b):
    return lax.dot_general(a, b, (((0,), (0,)), ((), ())), preferred_element_type=F32)


def _split3(x):
    hi = x.astype(BF16)
    r1 = x - hi.astype(F32)
    mid = r1.astype(BF16)
    lo = (r1 - mid.astype(F32)).astype(BF16)
    return hi, mid, lo


def _rms(x):
    return x * lax.rsqrt(jnp.mean(x * x, axis=-1, keepdims=True) + EPS)


def _silu(x):
    return x * jax.nn.sigmoid(x)


def _pick_tile(n, pref):
    if n <= pref:
        return n
    for t in range(pref, 7, -1):
        if n % t == 0 and t % 8 == 0:
            return t
    raise ValueError(f"no tile for {n}")


def _params(*sem):
    return pltpu.CompilerParams(dimension_semantics=sem, vmem_limit_bytes=VMEM_LIMIT)


def _inproj_kernel(x_ref, nw_ref, w_ref, z_ref, xn_ref):
    @pl.when(pl.program_id(1) == 0)
    def _():
        xn_ref[...] = (_rms(x_ref[...]) * nw_ref[...]).astype(BF16)

    z_ref[...] = _nn(xn_ref[...], w_ref[...])


def _in_proj(x, norm_w, w_z):
    n = x.shape[0]
    tm = _pick_tile(n, 1024)
    return pl.pallas_call(
        _inproj_kernel,
        grid=(n // tm, Z_WIDTH // Z_TILE),
        in_specs=[
            pl.BlockSpec((tm, D_MODEL), lambda i, j: (i, 0)),
            pl.BlockSpec((1, D_MODEL), lambda i, j: (0, 0)),
            pl.BlockSpec((D_MODEL, Z_TILE), lambda i, j: (0, j)),
        ],
        out_specs=pl.BlockSpec((tm, Z_TILE), lambda i, j: (i, j)),
        out_shape=jax.ShapeDtypeStruct((n, Z_WIDTH), F32),
        scratch_shapes=[pltpu.VMEM((tm, D_MODEL), BF16)],
        compiler_params=_params("parallel", "arbitrary"),
        name="in_proj",
    )(x, norm_w, w_z)


def _prep_kernel(zq_ref, zkv_ref, zkr_ref, cs_ref, qnw_ref, kvnw_ref, wq_ref, wuk_ref,
                 ql_ref, qr_ref, ckv_ref, kr_ref, ckvb_ref, krb_ref):
    cqn = (_rms(zq_ref[...]) * qnw_ref[...]).astype(BF16)
    q = _nn(cqn, wq_ref[...])
    cs = cs_ref[...]
    cos = cs[:, :ROPE]
    sin = cs[:, ROPE:]
    for h in range(N_HEADS):
        qn = q[:, h * HEAD_DIM:(h + 1) * HEAD_DIM].astype(BF16)
        ql_ref[h] = (_nn(qn, wuk_ref[h]) * SM_SCALE).astype(BF16)
        a = q[:, D_HALF + h * ROPE:D_HALF + (h + 1) * ROPE]
        b = q[:, D_HALF + N_HEADS * ROPE + h * ROPE:D_HALF + N_HEADS * ROPE + (h + 1) * ROPE]
        qr_ref[h] = ((a * cos + b * sin) * SM_SCALE).astype(BF16)
    ckv = _rms(zkv_ref[...]) * kvnw_ref[...]
    ckv_ref[...] = ckv
    ckvb_ref[...] = ckv.astype(BF16)
    prod = zkr_ref[...] * cs
    kr = prod[:, :ROPE] + prod[:, ROPE:]
    kr_ref[...] = kr
    krb_ref[...] = kr.astype(BF16)


def _qkv_prep(z, cs, q_norm_w, kv_norm_w, w_q, w_ukt):
    n = z.shape[0]
    tm = _pick_tile(n, 256)
    const2 = lambda i: (0, 0)
    return pl.pallas_call(
        _prep_kernel,
        grid=(n // tm,),
        in_specs=[
            pl.BlockSpec((tm, Q_LORA), lambda i: (i, 0)),
            pl.BlockSpec((tm, KV_LORA), lambda i: (i, 1)),
            pl.BlockSpec((tm, LANES), lambda i: (i, Z_ROPE_BLOCK)),
            pl.BlockSpec((tm, LANES), lambda i: (i, 0)),
            pl.BlockSpec((1, Q_LORA), const2),
            pl.BlockSpec((1, KV_LORA), const2),
            pl.BlockSpec((Q_LORA, 2 * D_HALF), const2),
            pl.BlockSpec((N_HEADS, HEAD_DIM, KV_LORA), lambda i: (0, 0, 0)),
        ],
        out_specs=[
            pl.BlockSpec((N_HEADS, tm, KV_LORA), lambda i: (0, i, 0)),
            pl.BlockSpec((N_HEADS, tm, ROPE), lambda i: (0, i, 0)),
            pl.BlockSpec((tm, KV_LORA), lambda i: (i, 0)),
            pl.BlockSpec((tm, ROPE), lambda i: (i, 0)),
            pl.BlockSpec((tm, KV_LORA), lambda i: (i, 0)),
            pl.BlockSpec((tm, ROPE), lambda i: (i, 0)),
        ],
        out_shape=[
            jax.ShapeDtypeStruct((N_HEADS, n, KV_LORA), BF16),
            jax.ShapeDtypeStruct((N_HEADS, n, ROPE), BF16),
            jax.ShapeDtypeStruct((n, KV_LORA), F32),
            jax.ShapeDtypeStruct((n, ROPE), F32),
            jax.ShapeDtypeStruct((n, KV_LORA), BF16),
            jax.ShapeDtypeStruct((n, ROPE), BF16),
        ],
        compiler_params=_params("parallel"),
        name="qkv_prep",
    )(z, z, z, cs, q_norm_w, kv_norm_w, w_q, w_ukt)


def _attn_prompt_kernel(ql_ref, qr_ref, kc_ref, kr_ref, wuv_ref, o_ref, *, tq, tk, n_blocks):
    qi = pl.program_id(1)
    rows = N_HEADS * tq
    ql = ql_ref[...].reshape(rows, KV_LORA)
    qr = qr_ref[...].reshape(rows, ROPE)
    needed = (qi * tq + (tq - 1)) // tk + 1

    def prefix(nblk):
        width = LANES + nblk * tk
        kc = kc_ref[0:width, :]
        s = _nt(ql, kc) + _nt(qr, kr_ref[0:width, :])
        s_meta = s[:, :LANES]
        s_meta = jnp.where(lax.broadcasted_iota(jnp.int32, s_meta.shape, 1) < N_META, s_meta, NEG)
        s_last = s[:, width - tk:]
        row = lax.broadcasted_iota(jnp.int32, s_last.shape, 0)
        col = lax.broadcasted_iota(jnp.int32, s_last.shape, 1)
        s_last = jnp.where((nblk - 1) * tk + col <= qi * tq + (row & (tq - 1)), s_last, NEG)
        middle = [s[:, LANES:width - tk]] if nblk > 1 else []
        s = jnp.concatenate([s_meta] + middle + [s_last], axis=1)
        p = jnp.exp(s - jnp.max(s, axis=-1, keepdims=True))
        o = (_nn(p.astype(BF16), kc) / jnp.sum(p, axis=-1, keepdims=True)).astype(BF16)
        for h in range(N_HEADS):
            o_ref[:, h * HEAD_DIM:(h + 1) * HEAD_DIM] = _nn(o[h * tq:(h + 1) * tq], wuv_ref[h])

    for nblk in range(1, n_blocks + 1):
        pl.when(needed == nblk)(functools.partial(prefix, nblk))


def _attn_prompt(ql, qr, keys_c, keys_r, w_uvt, n_batch):
    seq = keys_c.shape[1] - LANES
    n = n_batch * seq
    tq = 128
    tk = min(512, seq)
    assert seq % tq == 0 and seq % tk == 0 and tq & (tq - 1) == 0
    qb = seq // tq
    return pl.pallas_call(
        functools.partial(_attn_prompt_kernel, tq=tq, tk=tk, n_blocks=seq // tk),
        grid=(n_batch, qb),
        in_specs=[
            pl.BlockSpec((N_HEADS, tq, KV_LORA), lambda b, i: (0, b * qb + i, 0)),
            pl.BlockSpec((N_HEADS, tq, ROPE), lambda b, i: (0, b * qb + i, 0)),
            pl.BlockSpec((None, LANES + seq, KV_LORA), lambda b, i: (b, 0, 0)),
            pl.BlockSpec((None, LANES + seq, ROPE), lambda b, i: (b, 0, 0)),
            pl.BlockSpec((N_HEADS, KV_LORA, HEAD_DIM), lambda b, i: (0, 0, 0)),
        ],
        out_specs=pl.BlockSpec((tq, D_HALF), lambda b, i: (b * qb + i, 0)),
        out_shape=jax.ShapeDtypeStruct((n, D_HALF), F32),
        compiler_params=_params("parallel", "parallel"),
        name="mla_prompt",
    )(ql, qr, keys_c, keys_r, w_uvt)


def _attn_decode_kernel(pt_ref, ql_ref, qr_ref, nc_ref, nr_ref, cc_hbm, cr_hbm, wuv_ref, o_ref,
                        cbuf, rbuf, sem, kc_sc, kr_sc, m_sc, l_sc, acc_sc,
                        *, chunk, n_chunks, n_batch, n_new, page):
    b = pl.program_id(0)
    total = n_batch * n_chunks
    ahead = DECODE_SLOTS - 1
    rows = N_HEADS * n_new
    ql = ql_ref[...]
    qr = qr_ref[...]

    def chunk_copies(g):
        bb = lax.div(g, jnp.int32(n_chunks))
        first_page = (g - bb * n_chunks) * chunk
        slot = lax.rem(g, jnp.int32(DECODE_SLOTS))
        copies = []
        for k in range(chunk):
            pg = pt_ref[bb, first_page + k]
            copies.append(pltpu.make_async_copy(cc_hbm.at[pg], cbuf.at[slot, k], sem.at[slot]))
            copies.append(pltpu.make_async_copy(cr_hbm.at[pg], rbuf.at[slot, k], sem.at[slot]))
        return copies

    @pl.when(b == 0)
    def _():
        for g in range(min(ahead, total)):
            for cp in chunk_copies(jnp.int32(g)):
                cp.start()

    qlf = ql.astype(F32)
    qrf = qr.astype(F32)
    nc = nc_ref[...]
    nr = nr_ref[...]
    tok = lax.rem(lax.broadcasted_iota(jnp.int32, (rows, 1), 0), n_new)
    cols = []
    for t in range(n_new):
        sc = (jnp.sum(qlf * nc[t:t + 1, :], axis=-1, keepdims=True)
              + jnp.sum(qrf * nr[t:t + 1, :], axis=-1, keepdims=True))
        cols.append(jnp.where(tok >= t, sc, NEG))
    m0 = functools.reduce(jnp.maximum, cols)
    ps = [jnp.exp(c - m0) for c in cols]
    m_sc[...] = m0
    l_sc[...] = functools.reduce(jnp.add, ps)
    acc_sc[...] = functools.reduce(jnp.add, [p * nc[t:t + 1, :] for t, p in enumerate(ps)])

    def chunk_body(c, carry):
        g = b * n_chunks + c
        slot = lax.rem(g, jnp.int32(DECODE_SLOTS))
        for cp in chunk_copies(g):
            cp.wait()

        @pl.when(g + ahead < total)
        def _():
            for cp in chunk_copies(g + ahead):
                cp.start()

        for k in range(chunk):
            kc_sc[k * page:(k + 1) * page, :] = cbuf[slot, k].astype(BF16)
            kr_sc[:, k * page:(k + 1) * page] = rbuf[slot, k].astype(BF16)
        kc = kc_sc[...]
        s = _nt(ql, kc) + _nn(qr, kr_sc[...])
        m_prev = m_sc[...]
        m_new = jnp.maximum(m_prev, jnp.max(s, axis=-1, keepdims=True))
        alpha = jnp.exp(m_prev - m_new)
        p = jnp.exp(s - m_new)
        l_sc[...] = alpha * l_sc[...] + jnp.sum(p, axis=-1, keepdims=True)
        acc_sc[...] = alpha * acc_sc[...] + _nn(p.astype(BF16), kc)
        m_sc[...] = m_new
        return carry

    lax.fori_loop(0, n_chunks, chunk_body, 0)

    o = (acc_sc[...] / l_sc[...]).astype(BF16)
    head = lax.broadcasted_iota(jnp.int32, (rows, 1), 0) // n_new
    y = jnp.zeros((rows, HEAD_DIM), F32)
    for h in range(N_HEADS):
        y = jnp.where(head == h, _nn(o, wuv_ref[h]), y)
    o_ref[...] = y


def _attn_decode(ql, qr, new_c, new_r, cache_c, cache_r, page_table, w_uvt):
    nb, rows, _ = ql.shape
    n_new = rows // N_HEADS
    total_pages = page_table.shape[1]
    chunk = min(PAGES_PER_CHUNK, total_pages)
    n_chunks = total_pages // chunk
    assert total_pages % chunk == 0
    page = cache_c.shape[1]

    row3 = lambda b, pt: (b, 0, 0)
    grid_spec = pltpu.PrefetchScalarGridSpec(
        num_scalar_prefetch=1,
        grid=(nb,),
        in_specs=[
            pl.BlockSpec((None, rows, KV_LORA), row3),
            pl.BlockSpec((None, rows, ROPE), row3),
            pl.BlockSpec((None, n_new, KV_LORA), row3),
            pl.BlockSpec((None, n_new, ROPE), row3),
            pl.BlockSpec(memory_space=pl.ANY),
            pl.BlockSpec(memory_space=pl.ANY),
            pl.BlockSpec((N_HEADS, KV_LORA, HEAD_DIM), lambda b, pt: (0, 0, 0)),
        ],
        out_specs=pl.BlockSpec((None, rows, HEAD_DIM), row3),
        scratch_shapes=[pltpu.VMEM((DECODE_SLOTS, chunk, page, KV_LORA), F32),
                        pltpu.VMEM((DECODE_SLOTS, chunk, ROPE, page), F32),
                        pltpu.SemaphoreType.DMA((DECODE_SLOTS,)),
                        pltpu.VMEM((chunk * page, KV_LORA), BF16),
                        pltpu.VMEM((ROPE, chunk * page), BF16),
                        pltpu.VMEM((rows, 1), F32), pltpu.VMEM((rows, 1), F32),
                        pltpu.VMEM((rows, KV_LORA), F32)],
    )
    return pl.pallas_call(
        functools.partial(_attn_decode_kernel, chunk=chunk, n_chunks=n_chunks, n_batch=nb, n_new=n_new,
                          page=page),
        grid_spec=grid_spec,
        out_shape=jax.ShapeDtypeStruct((nb, rows, HEAD_DIM), F32),
        compiler_params=_params("arbitrary"),
        name="mla_decode",
    )(page_table, ql, qr, new_c, new_r, cache_c, cache_r, w_uvt)


def _hgrn_gates(qb, fb, ib, gb, lb, keep):
    logf = jnp.log(lb + (1.0 - lb) * jax.nn.sigmoid(fb))
    kk = (1.0 - lb) * jax.nn.sigmoid(-fb)
    if keep is not None:
        logf = jnp.where(keep, logf, 0.0)
        kk = jnp.where(keep, kk, 0.0)
    return logf, kk, _silu(qb), ib, _silu(gb)


def _hgrn_kernel(qb_ref, fb_ref, ib_ref, gb_ref, lb_ref, nw_ref, s0_ref, y_ref, s_ref, st_sc,
                 *, nbt, chunk, l_valid, l_total, state_kv):
    c = pl.program_id(1)

    @pl.when(c == 0)
    def _():
        for nb in range(nbt):
            for h in range(N_HEADS):
                st_sc[nb, h] = s0_ref[nb, h].T if state_kv else s0_ref[nb, h]

    lb = lb_ref[...]
    nw = nw_ref[...]
    t_row = lax.broadcasted_iota(jnp.int32, (chunk, chunk), 0)
    t_col = lax.broadcasted_iota(jnp.int32, (chunk, chunk), 1)
    causal = t_row >= t_col
    tri = jnp.where(causal, 1.0, 0.0).astype(BF16)
    tok = lax.broadcasted_iota(jnp.int32, (chunk, 1), 0)
    keep = (c * chunk + tok < l_valid) if l_valid < l_total else None

    gates = [_hgrn_gates(qb_ref[nb], fb_ref[nb], ib_ref[nb], gb_ref[nb], lb, keep) for nb in range(nbt)]
    decay = functools.reduce(jnp.maximum, [-jnp.sum(g[0], axis=0, keepdims=True) for g in gates])
    safe = jnp.max(decay) < HGRN_SAFE_DECAY

    def heads(h):
        return slice(h * HEAD_DIM, (h + 1) * HEAD_DIM)

    @pl.when(safe)
    def _():
        for nb in range(nbt):
            logf, kk, q, v, gate = gates[nb]
            hi, mid, lo = _split3(logf)
            b = _nn(tri, hi) + _nn(tri, mid) + _nn(tri, lo)
            b_end = b[chunk - 1:chunk, :]
            qd = (q * jnp.exp(b)).astype(BF16)
            ke = (kk * jnp.exp(-b)).astype(BF16)
            wd = (kk * jnp.exp(b_end - b)).astype(BF16)
            e_end = jnp.exp(b_end)
            vb = v.astype(BF16)
            for h in range(N_HEADS):
                hs = heads(h)
                a = jnp.where(causal, _nt(qd[:, hs], ke[:, hs]), 0.0).astype(BF16)
                st = st_sc[nb, h]
                o = _nn(a, vb[:, hs]) + _nt(qd[:, hs], st.astype(BF16))
                st_sc[nb, h] = st * e_end[:, hs] + _tn(vb[:, hs], wd[:, hs])
                y_ref[nb, :, hs] = (_rms(o) * nw[:, hs] * gate[:, hs]).astype(BF16)

    @pl.when(jnp.logical_not(safe))
    def _():
        def batch_body(nb, carry):
            logf, kk, q, v, gate = _hgrn_gates(qb_ref[nb], fb_ref[nb], ib_ref[nb], gb_ref[nb], lb, keep)
            for h in range(N_HEADS):
                hs = heads(h)
                lf, qh, vh = logf[:, hs], q[:, hs], v[:, hs]
                kh = kk[:, hs].astype(BF16)

                def token(t, carry_t):
                    st, o = carry_t
                    here = tok == t
                    f_row = jnp.exp(jnp.sum(jnp.where(here, lf, 0.0), axis=0, keepdims=True))
                    st = st * f_row + _tn(jnp.where(here, vh, 0.0).astype(BF16), kh)
                    o = o + _nt(jnp.where(here, qh, 0.0).astype(BF16), st.astype(BF16))
                    return st, o

                st, o = lax.fori_loop(0, chunk, token,
                                      (st_sc[nb, h], jnp.zeros((chunk, HEAD_DIM), F32)))
                st_sc[nb, h] = st
                y_ref[nb, :, hs] = (_rms(o) * nw[:, hs] * gate[:, hs]).astype(BF16)
            return carry

        lax.fori_loop(0, nbt, batch_body, 0)

    @pl.when(c == pl.num_programs(1) - 1)
    def _():
        for nb in range(nbt):
            for h in range(N_HEADS):
                s_ref[nb, h] = st_sc[nb, h].T if state_kv else st_sc[nb, h]


def _hgrn(z3, lb, norm_w, s0, *, chunk, nbt, l_valid, state_kv):
    nb, l_total, _ = z3.shape
    assert l_total % chunk == 0 and nb % nbt == 0

    def zspec(k):
        return pl.BlockSpec((nbt, chunk, D_HALF), lambda b, c: (b, c, k))

    sshape = (nbt, N_HEADS, HEAD_DIM, HEAD_DIM)
    sspec = pl.BlockSpec(sshape, lambda b, c: (b, 0, 0, 0))
    vec = pl.BlockSpec((1, D_HALF), lambda b, c: (0, 0))
    return pl.pallas_call(
        functools.partial(_hgrn_kernel, nbt=nbt, chunk=chunk, l_valid=l_valid, l_total=l_total,
                          state_kv=state_kv),
        grid=(nb // nbt, l_total // chunk),
        in_specs=[zspec(1), zspec(2), zspec(3), zspec(4), vec, vec, sspec],
        out_specs=[pl.BlockSpec((nbt, chunk, D_HALF), lambda b, c: (b, c, 0)), sspec],
        out_shape=[jax.ShapeDtypeStruct((nb, l_total, D_HALF), BF16),
                   jax.ShapeDtypeStruct(s0.shape, F32)],
        scratch_shapes=[pltpu.VMEM(sshape, F32)],
        compiler_params=_params("parallel", "arbitrary"),
        name="hgrn",
    )(z3, z3, z3, z3, lb, norm_w, s0)


def _outproj_router_kernel(hp_ref, hs_ref, yap_ref, yas_ref, ybp_ref, ybs_ref, anw_ref, wout_ref,
                           fnw_ref, wrh_ref, wrl_ref, br_ref,
                           h2_ref, xf_ref, slab_ref, slabt_ref, cnt_ref, *, tiles_p, tm):
    i = pl.program_id(0)
    is_p = i < tiles_p
    h = jnp.where(is_p, hp_ref[...], hs_ref[...])
    ya = jnp.where(is_p, yap_ref[...], yas_ref[...])
    yb = jnp.where(is_p, ybp_ref[...], ybs_ref[...])
    ya = (_rms(ya) * anw_ref[...]).astype(BF16)
    h2 = h + (_nn(ya, wout_ref[:D_HALF, :]) + _nn(yb, wout_ref[D_HALF:, :]))
    h2_ref[...] = h2
    xf = _rms(h2) * fnw_ref[...]
    for s in range(X_SLABS):
        xf_ref[pl.ds(s, tm, stride=X_SLABS), :] = xf[:, s * LANES:(s + 1) * LANES]
    x_hi = xf.astype(BF16)
    x_lo = (xf - x_hi.astype(F32)).astype(BF16)
    w_hi = wrh_ref[...]
    logits = _nn(x_hi, w_hi) + _nn(x_lo, w_hi) + _nn(x_hi, wrl_ref[...]) + br_ref[...]
    lane = lax.broadcasted_iota(jnp.int32, logits.shape, 1)

    def first_max(x):
        m = jnp.max(x, axis=-1, keepdims=True)
        return m, jnp.min(jnp.where(x == m, lane, LANES), axis=-1, keepdims=True)

    gl = jnp.where(lane < N_GROUPS, logits, NEG)
    g_max, g_idx = first_max(gl)
    pg_top = 1.0 / jnp.sum(jnp.exp(gl - g_max), axis=-1, keepdims=True)
    lo = N_GROUPS + EXP_PER_GROUP * g_idx
    el = jnp.where((lane >= lo) & (lane < lo + EXP_PER_GROUP), logits, NEG)
    e1, i1 = first_max(el)
    e2, i2 = first_max(jnp.where(lane == i1, NEG, el))
    r = jnp.exp(e2 - e1)
    w1 = pg_top / (1.0 + r)
    w2 = w1 * r

    @pl.when(i == 0)
    def _():
        cnt_ref[...] = jnp.zeros_like(cnt_ref)

    onehot = jnp.where((lane == i1) | (lane == i2), 1.0, 0.0)
    t_row = lax.broadcasted_iota(jnp.int32, (tm, tm), 0)
    t_col = lax.broadcasted_iota(jnp.int32, (tm, tm), 1)
    before = jnp.where(t_row > t_col, 1.0, 0.0).astype(BF16)
    prefix = _nn(before, onehot.astype(BF16)) + cnt_ref[0:1, :]
    r1 = jnp.sum(jnp.where(lane == i1, prefix, 0.0), axis=-1, keepdims=True)
    r2 = jnp.sum(jnp.where(lane == i2, prefix, 0.0), axis=-1, keepdims=True)
    cnt_ref[...] = cnt_ref[...] + jnp.sum(onehot, axis=0, keepdims=True)
    cols = [(i1 - N_GROUPS).astype(F32), (i2 - N_GROUPS).astype(F32), w1, w2, r1, r2]
    slab = jnp.zeros(logits.shape, F32)
    for k, val in enumerate(cols):
        slab = jnp.where(lane == k, val, slab)
    slab_ref[...] = slab
    slabt_ref[...] = slab.T[:8, :]


def _outproj_router(h_p, h_s, ya_p, ya_s, yb_p, yb_s, a_norm_w, w_out, f_norm_w, wr_hi, wr_lo, b_r):
    n_p, n_s = h_p.shape[0], h_s.shape[0]
    tm = _pick_tile(n_s, 256)
    assert n_p % tm == 0
    tiles_p, tiles_s = n_p // tm, n_s // tm
    n = n_p + n_s
    pmap = lambda i: (jnp.minimum(i, tiles_p - 1), 0)
    smap = lambda i: (jnp.maximum(i - tiles_p, 0), 0)
    const = lambda i: (0, 0)
    row = lambda i: (i, 0)
    return pl.pallas_call(
        functools.partial(_outproj_router_kernel, tiles_p=tiles_p, tm=tm),
        grid=(tiles_p + tiles_s,),
        in_specs=[
            pl.BlockSpec((tm, D_MODEL), pmap), pl.BlockSpec((tm, D_MODEL), smap),
            pl.BlockSpec((tm, D_HALF), pmap), pl.BlockSpec((tm, D_HALF), smap),
            pl.BlockSpec((tm, D_HALF), pmap), pl.BlockSpec((tm, D_HALF), smap),
            pl.BlockSpec((1, D_HALF), const),
            pl.BlockSpec((D_MODEL, D_MODEL), const),
            pl.BlockSpec((1, D_MODEL), const),
            pl.BlockSpec((D_MODEL, LANES), const), pl.BlockSpec((D_MODEL, LANES), const),
            pl.BlockSpec((1, LANES), const),
        ],
        out_specs=[pl.BlockSpec((tm, D_MODEL), row), pl.BlockSpec((tm * X_SLABS, LANES), row),
                   pl.BlockSpec((tm, LANES), row), pl.BlockSpec((8, tm), lambda i: (0, i)),
                   pl.BlockSpec((8, LANES), const)],
        out_shape=[jax.ShapeDtypeStruct((n, D_MODEL), F32), jax.ShapeDtypeStruct((n * X_SLABS, LANES), F32),
                   jax.ShapeDtypeStruct((n, LANES), F32), jax.ShapeDtypeStruct((8, n), F32),
                   jax.ShapeDtypeStruct((8, LANES), F32)],
        compiler_params=_params("arbitrary"),
        name="outproj_router",
    )(h_p, h_s, ya_p, ya_s, yb_p, yb_s, a_norm_w, w_out, f_norm_w, wr_hi, wr_lo, b_r)


def _expert_kernel(te_ref, tv_ref, tr_ref, tok_ref, x_hbm, wg_ref, wu_ref, wd_ref, y_ref,
                   xbuf, sem, wg_sc, wu_sc, wd_sc):
    t = pl.program_id(0)
    n_tiles = pl.num_programs(0)
    slot = lax.rem(t, 2)

    def row_copy(tile, r, s):
        tok = tok_ref[tile * MOE_TILE + r]
        return pltpu.make_async_copy(x_hbm.at[tok], xbuf.at[s, pl.ds(r * X_SLABS, X_SLABS)], sem.at[s])

    def row_groups(tile):
        return (tr_ref[tile] + (GATHER_UNROLL - 1)) // GATHER_UNROLL

    def start_tile(tile, s):
        def body(i, carry):
            for j in range(GATHER_UNROLL):
                row_copy(tile, i * GATHER_UNROLL + j, s).start()
            return carry
        lax.fori_loop(0, row_groups(tile), body, 0)

    @pl.when(t == 0)
    def _():
        xbuf[...] = jnp.zeros_like(xbuf)
        start_tile(0, 0)

    nxt = jnp.minimum(t + 1, n_tiles - 1)

    @pl.when((t + 1 < n_tiles) & (tv_ref[nxt] == 1))
    def _():
        start_tile(nxt, 1 - slot)

    @pl.when((t == 0) | (te_ref[t] != te_ref[jnp.maximum(t - 1, 0)]))
    def _():
        wg_sc[...] = wg_ref[...].astype(BF16)
        wu_sc[...] = wu_ref[...].astype(BF16)
        wd_sc[...] = wd_ref[...].astype(BF16)

    @pl.when(tv_ref[t] == 1)
    def _():
        def body(i, carry):
            for j in range(GATHER_UNROLL):
                row_copy(t, i * GATHER_UNROLL + j, slot).wait()
            return carry
        lax.fori_loop(0, row_groups(t), body, 0)
        x = jnp.concatenate([xbuf[slot, pl.ds(s, MOE_TILE, stride=X_SLABS), :] for s in range(X_SLABS)],
                            axis=1).astype(BF16)
        hdn = (_silu(_nn(x, wg_sc[...])) * _nn(x, wu_sc[...])).astype(BF16)
        y_ref[...] = _nn(hdn, wd_sc[...])

    @pl.when(tv_ref[t] == 0)
    def _():
        y_ref[...] = jnp.zeros_like(y_ref)


def _expert_ffn(tile_expert, tile_valid, tile_rows, token_of_row, xf3, w_gate, w_up, w_down):
    n_rows = token_of_row.shape[0]
    grid_spec = pltpu.PrefetchScalarGridSpec(
        num_scalar_prefetch=4,
        grid=(n_rows // MOE_TILE,),
        in_specs=[
            pl.BlockSpec(memory_space=pl.ANY),
            pl.BlockSpec((None, D_MODEL, D_FF), lambda t, te, tv, tr, tok: (te[t], 0, 0)),
            pl.BlockSpec((None, D_MODEL, D_FF), lambda t, te, tv, tr, tok: (te[t], 0, 0)),
            pl.BlockSpec((None, D_FF, D_MODEL), lambda t, te, tv, tr, tok: (te[t], 0, 0)),
        ],
        out_specs=pl.BlockSpec((MOE_TILE, D_MODEL), lambda t, te, tv, tr, tok: (t, 0)),
        scratch_shapes=[pltpu.VMEM((2, MOE_TILE * X_SLABS, LANES), F32),
                        pltpu.SemaphoreType.DMA((2,)),
                        pltpu.VMEM((D_MODEL, D_FF), BF16), pltpu.VMEM((D_MODEL, D_FF), BF16),
                        pltpu.VMEM((D_FF, D_MODEL), BF16)],
    )
    return pl.pallas_call(
        _expert_kernel,
        grid_spec=grid_spec,
        out_shape=jax.ShapeDtypeStruct((n_rows, D_MODEL), F32),
        compiler_params=_params("arbitrary"),
        name="expert_ffn",
    )(tile_expert, tile_valid, tile_rows, token_of_row, xf3, w_gate, w_up, w_down)


def _final_kernel(h2_ref, y0_ref, y1_ref, slab_ref, nw_ref, op_ref, os_ref, *, tiles_p):
    i = pl.program_id(0)
    slab = slab_ref[...]
    h = h2_ref[...] + (slab[:, 2:3] * y0_ref[...] + slab[:, 3:4] * y1_ref[...])
    y = _rms(h) * nw_ref[...]

    @pl.when(i < tiles_p)
    def _():
        op_ref[...] = y

    @pl.when(i >= tiles_p)
    def _():
        os_ref[...] = y


def _final(h2, y_pair, slab, norm_w, n_p):
    n = h2.shape[0]
    n_s = n - n_p
    tm = _pick_tile(n_s, 256)
    tiles_p, tiles_s = n_p // tm, n_s // tm
    return pl.pallas_call(
        functools.partial(_final_kernel, tiles_p=tiles_p),
        grid=(tiles_p + tiles_s,),
        in_specs=[
            pl.BlockSpec((tm, D_MODEL), lambda i: (i, 0)),
            pl.BlockSpec((None, tm, D_MODEL), lambda i: (0, i, 0)),
            pl.BlockSpec((None, tm, D_MODEL), lambda i: (1, i, 0)),
            pl.BlockSpec((tm, LANES), lambda i: (i, 0)),
            pl.BlockSpec((1, D_MODEL), lambda i: (0, 0)),
        ],
        out_specs=[pl.BlockSpec((tm, D_MODEL), lambda i: (jnp.minimum(i, tiles_p - 1), 0)),
                   pl.BlockSpec((tm, D_MODEL), lambda i: (jnp.maximum(i - tiles_p, 0), 0))],
        out_shape=[jax.ShapeDtypeStruct((n_p, D_MODEL), F32), jax.ShapeDtypeStruct((n_s, D_MODEL), F32)],
        compiler_params=_params("arbitrary"),
        name="combine_final_norm",
    )(h2, y_pair, y_pair, slab, norm_w)


def _rope_table(pos):
    half = ROPE // 2
    inv_freq = jnp.power(ROPE_THETA, -jnp.arange(half, dtype=F32) / half)
    ang = pos.astype(F32)[:, None] * inv_freq[None, :]
    cos, sin = jnp.cos(ang), jnp.sin(ang)
    return jnp.concatenate([cos, cos, sin, sin], axis=-1)


def _rotate_half_cols(w):
    half = ROPE // 2
    return jnp.concatenate([-w[..., half:], w[..., :half]], axis=-1)


def kernel(x_prompt, x_sample, cache_ckv, cache_krope, state_hgrn, page_table, meta_tokens, attn_norm_w, w_in, q_norm_w, w_uq, kv_norm_w, w_uk, w_uv, mla_out_norm_w, hgrn_lb_logits, hgrn_out_norm_w, w_out, ffn_norm_w, w_router_group, b_router_group, w_router_expert, b_router_expert, w_gate, w_up, w_down, final_norm_w):
    n_batch, seq, _ = x_prompt.shape
    n_dec, n_new, _ = x_sample.shape
    depth = w_in.shape[0]
    assert depth == 1
    n_p, n_s = n_batch * seq, n_dec * n_new
    past = page_table.shape[1] * cache_ckv.shape[2]

    wi = w_in[0]
    c0 = Q_LORA + KV_LORA
    w_kr = wi[:, c0:c0 + ROPE]
    w_z = jnp.concatenate([wi[:, :c0], wi[:, c0 + ROPE:], w_kr, _rotate_half_cols(w_kr),
                           jnp.zeros((D_MODEL, LANES), F32)], axis=1).astype(BF16)
    uq = w_uq[0]
    uq_rope = uq[:, :, HEAD_DIM:]
    w_q = jnp.concatenate([uq[:, :, :HEAD_DIM].reshape(Q_LORA, D_HALF),
                           uq_rope.reshape(Q_LORA, N_HEADS * ROPE),
                           _rotate_half_cols(uq_rope).reshape(Q_LORA, N_HEADS * ROPE)], axis=1).astype(BF16)
    w_ukt = jnp.transpose(w_uk[0], (1, 2, 0)).astype(BF16)
    w_uvt = jnp.transpose(w_uv[0], (1, 0, 2)).astype(BF16)
    w_o = w_out[0].astype(BF16)
    w_r = jnp.concatenate([w_router_group[0], w_router_expert[0],
                           jnp.zeros((D_MODEL, LANES - N_GROUPS - N_EXPERTS), F32)], axis=1)
    wr_hi = w_r.astype(BF16)
    wr_lo = (w_r - wr_hi.astype(F32)).astype(BF16)
    b_r = jnp.concatenate([b_router_group[0], b_router_expert[0],
                           jnp.zeros((LANES - N_GROUPS - N_EXPERTS,), F32)])[None, :]
    lb =jax.nn.softmax(hgrn_lb_logits.astype(F32), axis=0)[0][None, :]
    a_nw, q_nw, kv_nw = attn_norm_w[0][None, :], q_norm_w[0][None, :], kv_norm_w[0][None, :]
    mo_nw, ho_nw, f_nw = mla_out_norm_w[0][None, :], hgrn_out_norm_w[0][None, :], ffn_norm_w[0][None, :]

    z_m = _in_proj(meta_tokens, a_nw, w_z)
    _, _, ckv_m, kr_m, ckvb_m, krb_m = _qkv_prep(z_m, _rope_table(jnp.arange(N_META)), q_nw, kv_nw, w_q, w_ukt)
    _, s_meta = _hgrn(z_m[None], lb, ho_nw, jnp.zeros((1, N_HEADS, HEAD_DIM, HEAD_DIM), F32),
                      chunk=N_META, nbt=1, l_valid=N_META, state_kv=False)
    meta_c = jnp.pad(ckvb_m, ((0, LANES - N_META), (0, 0)))
    meta_r = jnp.pad(krb_m, ((0, LANES - N_META), (0, 0)))

    xp = x_prompt.reshape(n_p, D_MODEL)
    z_p = _in_proj(xp, a_nw, w_z)
    cs_p = jnp.tile(_rope_table(N_META + jnp.arange(seq)), (n_batch, 1))
    ql_p, qr_p, ckv_p, kr_p, ckvb_p, krb_p = _qkv_prep(z_p, cs_p, q_nw, kv_nw, w_q, w_ukt)
    def keys_with_meta(m, p, d):
        return jnp.concatenate([jnp.broadcast_to(m[None], (n_batch, LANES, d)), p.reshape(n_batch, seq, d)], axis=1)

    ya_p = _attn_prompt(ql_p, qr_p, keys_with_meta(meta_c, ckvb_p, KV_LORA),
                        keys_with_meta(meta_r, krb_p, ROPE), w_uvt, n_batch)
    s0_p = jnp.broadcast_to(s_meta, (n_batch, N_HEADS, HEAD_DIM, HEAD_DIM))
    yb_p, st_p = _hgrn(z_p.reshape(n_batch, seq, Z_WIDTH), lb, ho_nw, s0_p,
                       chunk=min(HGRN_CHUNK, seq), nbt=n_batch, l_valid=seq, state_kv=False)
    st_p = jnp.swapaxes(st_p, -1, -2)

    xs = x_sample.reshape(n_s, D_MODEL)
    z_s = _in_proj(xs, a_nw, w_z)
    cs_s = jnp.tile(_rope_table(past + jnp.arange(n_new)), (n_dec, 1))
    ql_s, qr_s, ckv_s, kr_s, _, _ = _qkv_prep(z_s, cs_s, q_nw, kv_nw, w_q, w_ukt)

    def dec_rows(a):
        d = a.shape[-1]
        return jnp.transpose(a.reshape(N_HEADS, n_dec, n_new, d), (1, 0, 2, 3)).reshape(n_dec, N_HEADS * n_new, d)

    ya_s = _attn_decode(dec_rows(ql_s), dec_rows(qr_s), ckv_s.reshape(n_dec, n_new, KV_LORA),
                        kr_s.reshape(n_dec, n_new, ROPE), cache_ckv[0],
                        jnp.swapaxes(cache_krope[0], 1, 2), page_table, w_uvt)
    ya_s = jnp.transpose(ya_s.reshape(n_dec, N_HEADS, n_new, HEAD_DIM), (0, 2, 1, 3)).reshape(n_s, D_HALF)
    pad_new = -n_new % 8
    z_s3 = jnp.pad(z_s.reshape(n_dec, n_new, Z_WIDTH), ((0, 0), (0, pad_new), (0, 0)))
    yb_s, st_s = _hgrn(z_s3, lb, ho_nw, state_hgrn[0].astype(F32), chunk=n_new + pad_new,
                       nbt=math.gcd(n_dec, 4), l_valid=n_new, state_kv=True)
    yb_s = yb_s[:, :n_new].reshape(n_s, D_HALF)

    h2, xf, slab, slab_t, cnt = _outproj_router(xp, xs, ya_p, ya_s, yb_p.reshape(n_p, D_HALF), yb_s,
                                        mo_nw, w_o, f_nw, wr_hi, wr_lo, b_r)

    n = n_p + n_s
    routed = slab_t.astype(jnp.int32)
    counts = cnt[0, N_GROUPS:N_GROUPS + N_EXPERTS].astype(jnp.int32)
    padded = (counts + MOE_TILE - 1) // MOE_TILE * MOE_TILE
    ends = jnp.cumsum(padded)
    first = ends - padded
    pos = jnp.concatenate([first[routed[0]] + routed[4], first[routed[1]] + routed[5]])
    n_tiles = -(-2 * n // MOE_TILE) + N_EXPERTS
    starts = jnp.arange(n_tiles, dtype=jnp.int32) * MOE_TILE
    tile_valid = (starts < ends[-1]).astype(jnp.int32)
    last_valid = jnp.maximum(ends[-1] // MOE_TILE - 1, 0)
    tile_expert = jnp.sum((ends[None, :] <= jnp.minimum(starts, last_valid * MOE_TILE)[:, None])
                          .astype(jnp.int32), axis=1)
    tile_rows = jnp.clip(counts[tile_expert] - (starts - first[tile_expert]), 0, MOE_TILE) * tile_valid
    tokens = jnp.arange(n, dtype=jnp.int32)
    token_of_row = jnp.zeros((n_tiles * MOE_TILE,), jnp.int32).at[pos].set(jnp.concatenate([tokens, tokens]))
    y_sorted = _expert_ffn(tile_expert, tile_valid, tile_rows, token_of_row, xf.reshape(n, X_SLABS, LANES),
                           w_gate[0], w_up[0], w_down[0])
    y_slots = y_sorted.at[pos].get(mode="promise_in_bounds").reshape(2, n, D_MODEL)

    y_p, y_s = _final(h2, y_slots, slab, final_norm_w[None, :], n_p)

    def with_meta(m, p, d):
        return jnp.concatenate([jnp.broadcast_to(m[None], (n_batch, N_META, d)), p.reshape(n_batch, seq, d)], axis=1)[None]

    return (y_p.reshape(n_batch, seq, D_MODEL),
            y_s.reshape(n_dec, n_new, D_MODEL),
            with_meta(ckv_m, ckv_p, KV_LORA),
            with_meta(kr_m, kr_p, ROPE),
            st_p[None].astype(x_prompt.dtype),
            ckv_s.reshape(n_dec, n_new, KV_LORA)[None],
            kr_s.reshape(n_dec, n_new, ROPE)[None],
            st_s[None].astype(state_hgrn.dtype))
```

```python
import functools
import math

import jax
import jax.numpy as jnp
from jax import lax
from jax.experimental import pallas as pl
from jax.experimental.pallas import tpu as pltpu

F32 = jnp.float32
BF16 = jnp.bfloat16

D_MODEL = 2048
N_META = 16
N_HEADS = 8
HEAD_DIM = 128
ROPE = 64
Q_LORA = 512
KV_LORA = 512
D_HALF = N_HEADS * HEAD_DIM
ROPE_THETA = 10000.0
SM_SCALE = (HEAD_DIM + ROPE) ** -0.5
EPS = 1e-6
N_GROUPS = 4
EXP_PER_GROUP = 8
N_EXPERTS = N_GROUPS * EXP_PER_GROUP
D_FF = 512

Z_WIDTH = 5376
Z_TILE = 1792
Z_ROPE_BLOCK = 40

LANES = 128
VMEM_LIMIT = 56 * 1024 * 1024
NEG = -1e30
HGRN_SAFE_DECAY = 60.0
HGRN_CHUNK = 128
MOE_TILE = 256
X_SLABS = D_MODEL // LANES
GATHER_UNROLL = 8
PAGES_PER_CHUNK = 32
DECODE_SLOTS = 3


def _nn(a, b):
    return jnp.dot(a, b, preferred_element_type=F32)


def _nt(a, b):
    return lax.dot_general(a, b, (((1,), (1,)), ((), ())), preferred_element_type=F32)


def _tn(a, b):
    return lax.dot_general(a, b, (((0,), (0,)), ((), ())), preferred_element_type=F32)


def _split3(x):
    hi = x.astype(BF16)
    r1 = x - hi.astype(F32)
    mid = r1.astype(BF16)
    lo = (r1 - mid.astype(F32)).astype(BF16)
    return hi, mid, lo


def _rms(x):
    return x * lax.rsqrt(jnp.mean(x * x, axis=-1, keepdims=True) + EPS)


def _silu(x):
    return x * jax.nn.sigmoid(x)


def _pick_tile(n, pref):
    if n <= pref:
        return n
    for t in range(pref, 7, -1):
        if n % t == 0 and t % 8 == 0:
            return t
    raise ValueError(f"no tile for {n}")


def _params(*sem):
    return pltpu.CompilerParams(dimension_semantics=sem, vmem_limit_bytes=VMEM_LIMIT)


def _inproj_kernel(x_ref, nw_ref, w_ref, z_ref, xn_ref):
    @pl.when(pl.program_id(1) == 0)
    def _():
        xn_ref[...] = (_rms(x_ref[...]) * nw_ref[...]).astype(BF16)

    z_ref[...] = _nt(xn_ref[...], w_ref[...])


def _in_proj(x, norm_w, w_zt):
    n = x.shape[0]
    tm = _pick_tile(n, 1024)
    return pl.pallas_call(
        _inproj_kernel,
        grid=(n // tm, Z_WIDTH // Z_TILE),
        in_specs=[
            pl.BlockSpec((tm, D_MODEL), lambda i, j: (i, 0)),
            pl.BlockSpec((1, D_MODEL), lambda i, j: (0, 0)),
            pl.BlockSpec((Z_TILE, D_MODEL), lambda i, j: (j, 0)),
        ],
        out_specs=pl.BlockSpec((tm, Z_TILE), lambda i, j: (i, j)),
        out_shape=jax.ShapeDtypeStruct((n, Z_WIDTH), F32),
        scratch_shapes=[pltpu.VMEM((tm, D_MODEL), BF16)],
        compiler_params=_params("parallel", "arbitrary"),
        name="in_proj",
    )(x, norm_w, w_zt)


def _prep_kernel(zq_ref, zkv_ref, zkr_ref, cs_ref, qnw_ref, kvnw_ref, wq_ref, wuk_ref,
                 ql_ref, qr_ref, ckv_ref, kr_ref, ckvb_ref, krb_ref):
    cqn = (_rms(zq_ref[...]) * qnw_ref[...]).astype(BF16)
    q = _nn(cqn, wq_ref[...])
    cs = cs_ref[...]
    cos = cs[:, :ROPE]
    sin = cs[:, ROPE:]
    for h in range(N_HEADS):
        qn = q[:, h * HEAD_DIM:(h + 1) * HEAD_DIM].astype(BF16)
        ql_ref[h] = (_nn(qn, wuk_ref[h]) * SM_SCALE).astype(BF16)
        a = q[:, D_HALF + h * ROPE:D_HALF + (h + 1) * ROPE]
        b = q[:, D_HALF + N_HEADS * ROPE + h * ROPE:D_HALF + N_HEADS * ROPE + (h + 1) * ROPE]
        qr_ref[h] = ((a * cos + b * sin) * SM_SCALE).astype(BF16)
    ckv = _rms(zkv_ref[...]) * kvnw_ref[...]
    ckv_ref[...] = ckv
    ckvb_ref[...] = ckv.astype(BF16)
    prod = zkr_ref[...] * cs
    kr = prod[:, :ROPE] + prod[:, ROPE:]
    kr_ref[...] = kr
    krb_ref[...] = kr.astype(BF16)


def _qkv_prep(z, cs, q_norm_w, kv_norm_w, w_q, w_ukt):
    n = z.shape[0]
    tm = _pick_tile(n, 256)
    const2 = lambda i: (0, 0)
    return pl.pallas_call(
        _prep_kernel,
        grid=(n // tm,),
        in_specs=[
            pl.BlockSpec((tm, Q_LORA), lambda i: (i, 0)),
            pl.BlockSpec((tm, KV_LORA), lambda i: (i, 1)),
            pl.BlockSpec((tm, LANES), lambda i: (i, Z_ROPE_BLOCK)),
            pl.BlockSpec((tm, LANES), lambda i: (i, 0)),
            pl.BlockSpec((1, Q_LORA), const2),
            pl.BlockSpec((1, KV_LORA), const2),
            pl.BlockSpec((Q_LORA, 2 * D_HALF), const2),
            pl.BlockSpec((N_HEADS, HEAD_DIM, KV_LORA), lambda i: (0, 0, 0)),
        ],
        out_specs=[
            pl.BlockSpec((N_HEADS, tm, KV_LORA), lambda i: (0, i, 0)),
            pl.BlockSpec((N_HEADS, tm, ROPE), lambda i: (0, i, 0)),
            pl.BlockSpec((tm, KV_LORA), lambda i: (i, 0)),
            pl.BlockSpec((tm, ROPE), lambda i: (i, 0)),
            pl.BlockSpec((tm, KV_LORA), lambda i: (i, 0)),
            pl.BlockSpec((tm, ROPE), lambda i: (i, 0)),
        ],
        out_shape=[
            jax.ShapeDtypeStruct((N_HEADS, n, KV_LORA), BF16),
            jax.ShapeDtypeStruct((N_HEADS, n, ROPE), BF16),
            jax.ShapeDtypeStruct((n, KV_LORA), F32),
            jax.ShapeDtypeStruct((n, ROPE), F32),
            jax.ShapeDtypeStruct((n, KV_LORA), BF16),
            jax.ShapeDtypeStruct((n, ROPE), BF16),
        ],
        compiler_params=_params("parallel"),
        name="qkv_prep",
    )(z, z, z, cs, q_norm_w, kv_norm_w, w_q, w_ukt)


def _attn_prompt_kernel(ql_ref, qr_ref, kc_ref, kr_ref, wuv_ref, o_ref, *, tq, tk, n_blocks):
    qi = pl.program_id(1)
    rows = N_HEADS * tq
    ql = ql_ref[...].reshape(rows, KV_LORA)
    qr = qr_ref[...].reshape(rows, ROPE)
    needed = (qi * tq + (tq - 1)) // tk + 1

    def prefix(nblk):
        width = LANES + nblk * tk
        kc = kc_ref[0:width, :]
        s = _nt(ql, kc) + _nt(qr, kr_ref[0:width, :])
        s_meta = s[:, :LANES]
        s_meta = jnp.where(lax.broadcasted_iota(jnp.int32, s_meta.shape, 1) < N_META, s_meta, NEG)
        s_last = s[:, width - tk:]
        row = lax.broadcasted_iota(jnp.int32, s_last.shape, 0)
        col = lax.broadcasted_iota(jnp.int32, s_last.shape, 1)
        s_last = jnp.where((nblk - 1) * tk + col <= qi * tq + (row & (tq - 1)), s_last, NEG)
        middle = [s[:, LANES:width - tk]] if nblk > 1 else []
        s = jnp.concatenate([s_meta] + middle + [s_last], axis=1)
        p = jnp.exp(s - jnp.max(s, axis=-1, keepdims=True))
        o = (_nn(p.astype(BF16), kc) / jnp.sum(p, axis=-1, keepdims=True)).astype(BF16)
        for h in range(N_HEADS):
            o_ref[:, h * HEAD_DIM:(h + 1) * HEAD_DIM] = _nn(o[h * tq:(h + 1) * tq], wuv_ref[h])

    for nblk in range(1, n_blocks + 1):
        pl.when(needed == nblk)(functools.partial(prefix, nblk))


def _attn_prompt(ql, qr, keys_c, keys_r, w_uvt, n_batch):
    seq = keys_c.shape[1] - LANES
    n = n_batch * seq
    tq = 128
    tk = min(512, seq)
    assert seq % tq == 0 and seq % tk == 0 and tq & (tq - 1) == 0
    qb = seq // tq
    return pl.pallas_call(
        functools.partial(_attn_prompt_kernel, tq=tq, tk=tk, n_blocks=seq // tk),
        grid=(n_batch, qb),
        in_specs=[
            pl.BlockSpec((N_HEADS, tq, KV_LORA), lambda b, i: (0, b * qb + i, 0)),
            pl.BlockSpec((N_HEADS, tq, ROPE), lambda b, i: (0, b * qb + i, 0)),
            pl.BlockSpec((None, LANES + seq, KV_LORA), lambda b, i: (b, 0, 0)),
            pl.BlockSpec((None, LANES + seq, ROPE), lambda b, i: (b, 0, 0)),
            pl.BlockSpec((N_HEADS, KV_LORA, HEAD_DIM), lambda b, i: (0, 0, 0)),
        ],
        out_specs=pl.BlockSpec((tq, D_HALF), lambda b, i: (b * qb + i, 0)),
        out_shape=jax.ShapeDtypeStruct((n, D_HALF), F32),
        compiler_params=_params("parallel", "parallel"),
        name="mla_prompt",
    )(ql, qr, keys_c, keys_r, w_uvt)


def _attn_decode_kernel(pt_ref, ql_ref, qr_ref, nc_ref, nr_ref, cc_hbm, cr_hbm, wuv_ref, o_ref,
                        cbuf, rbuf, sem, kc_sc, kr_sc, m_sc, l_sc, acc_sc,
                        *, chunk, n_chunks, n_batch, n_new, page):
    b = pl.program_id(0)
    total = n_batch * n_chunks
    ahead = DECODE_SLOTS - 1
    rows = N_HEADS * n_new
    ql = ql_ref[...]
    qr = qr_ref[...]

    def chunk_copies(g):
        bb = lax.div(g, jnp.int32(n_chunks))
        first_page = (g - bb * n_chunks) * chunk
        slot = lax.rem(g, jnp.int32(DECODE_SLOTS))
        copies = []
        for k in range(chunk):
            pg = pt_ref[bb, first_page + k]
            copies.append(pltpu.make_async_copy(cc_hbm.at[pg], cbuf.at[slot, k], sem.at[slot]))
            copies.append(pltpu.make_async_copy(cr_hbm.at[pg], rbuf.at[slot, k], sem.at[slot]))
        return copies

    @pl.when(b == 0)
    def _():
        for g in range(min(ahead, total)):
            for cp in chunk_copies(jnp.int32(g)):
                cp.start()

    qlf = ql.astype(F32)
    qrf = qr.astype(F32)
    nc = nc_ref[...]
    nr = nr_ref[...]
    tok = lax.rem(lax.broadcasted_iota(jnp.int32, (rows, 1), 0), n_new)
    cols = []
    for t in range(n_new):
        sc = (jnp.sum(qlf * nc[t:t + 1, :], axis=-1, keepdims=True)
              + jnp.sum(qrf * nr[t:t + 1, :], axis=-1, keepdims=True))
        cols.append(jnp.where(tok >= t, sc, NEG))
    m0 = functools.reduce(jnp.maximum, cols)
    ps = [jnp.exp(c - m0) for c in cols]
    m_sc[...] = m0
    l_sc[...] = functools.reduce(jnp.add, ps)
    acc_sc[...] = functools.reduce(jnp.add, [p * nc[t:t + 1, :] for t, p in enumerate(ps)])

    def chunk_body(c, carry):
        g = b * n_chunks + c
        slot = lax.rem(g, jnp.int32(DECODE_SLOTS))
        for cp in chunk_copies(g):
            cp.wait()

        @pl.when(g + ahead < total)
        def _():
            for cp in chunk_copies(g + ahead):
                cp.start()

        for k in range(chunk):
            kc_sc[k * page:(k + 1) * page, :] = cbuf[slot, k].astype(BF16)
            kr_sc[:, k * page:(k + 1) * page] = rbuf[slot, k].astype(BF16)
        kc = kc_sc[...]
        s = _nt(ql, kc) + _nn(qr, kr_sc[...])
        m_prev = m_sc[...]
        m_new = jnp.maximum(m_prev, jnp.max(s, axis=-1, keepdims=True))
        alpha = jnp.exp(m_prev - m_new)
        p = jnp.exp(s - m_new)
        l_sc[...] = alpha * l_sc[...] + jnp.sum(p, axis=-1, keepdims=True)
        acc_sc[...] = alpha * acc_sc[...] + _nn(p.astype(BF16), kc)
        m_sc[...] = m_new
        return carry

    lax.fori_loop(0, n_chunks, chunk_body, 0)

    o = (acc_sc[...] / l_sc[...]).astype(BF16)
    head = lax.broadcasted_iota(jnp.int32, (rows, 1), 0) // n_new
    y = jnp.zeros((rows, HEAD_DIM), F32)
    for h in range(N_HEADS):
        y = jnp.where(head == h, _nn(o, wuv_ref[h]), y)
    o_ref[...] = y


def _attn_decode(ql, qr, new_c, new_r, cache_c, cache_r, page_table, w_uvt):
    nb, rows, _ = ql.shape
    n_new = rows // N_HEADS
    total_pages = page_table.shape[1]
    chunk = min(PAGES_PER_CHUNK, total_pages)
    n_chunks = total_pages // chunk
    assert total_pages % chunk == 0
    page = cache_c.shape[1]

    row3 = lambda b, pt: (b, 0, 0)
    grid_spec = pltpu.PrefetchScalarGridSpec(
        num_scalar_prefetch=1,
        grid=(nb,),
        in_specs=[
            pl.BlockSpec((None, rows, KV_LORA), row3),
            pl.BlockSpec((None, rows, ROPE), row3),
            pl.BlockSpec((None, n_new, KV_LORA), row3),
            pl.BlockSpec((None, n_new, ROPE), row3),
            pl.BlockSpec(memory_space=pl.ANY),
            pl.BlockSpec(memory_space=pl.ANY),
            pl.BlockSpec((N_HEADS, KV_LORA, HEAD_DIM), lambda b, pt: (0, 0, 0)),
        ],
        out_specs=pl.BlockSpec((None, rows, HEAD_DIM), row3),
        scratch_shapes=[pltpu.VMEM((DECODE_SLOTS, chunk, page, KV_LORA), F32),
                        pltpu.VMEM((DECODE_SLOTS, chunk, ROPE, page), F32),
                        pltpu.SemaphoreType.DMA((DECODE_SLOTS,)),
                        pltpu.VMEM((chunk * page, KV_LORA), BF16),
                        pltpu.VMEM((ROPE, chunk * page), BF16),
                        pltpu.VMEM((rows, 1), F32), pltpu.VMEM((rows, 1), F32),
                        pltpu.VMEM((rows, KV_LORA), F32)],
    )
    return pl.pallas_call(
        functools.partial(_attn_decode_kernel, chunk=chunk, n_chunks=n_chunks, n_batch=nb, n_new=n_new,
                          page=page),
        grid_spec=grid_spec,
        out_shape=jax.ShapeDtypeStruct((nb, rows, HEAD_DIM), F32),
        compiler_params=_params("arbitrary"),
        name="mla_decode",
    )(page_table, ql, qr, new_c, new_r, cache_c, cache_r, w_uvt)


def _hgrn_gates(qb, fb, ib, gb, lb, keep):
    logf = jnp.log(lb + (1.0 - lb) * jax.nn.sigmoid(fb))
    kk = (1.0 - lb) * jax.nn.sigmoid(-fb)
    if keep is not None:
        logf = jnp.where(keep, logf, 0.0)
        kk = jnp.where(keep, kk, 0.0)
    return logf, kk, _silu(qb), ib, _silu(gb)


def _hgrn_kernel(qb_ref, fb_ref, ib_ref, gb_ref, lb_ref, nw_ref, s0_ref, y_ref, s_ref, st_sc,
                 *, nbt, chunk, l_valid, l_total, state_kv):
    c = pl.program_id(1)

    @pl.when(c == 0)
    def _():
        for nb in range(nbt):
            for h in range(N_HEADS):
                st_sc[nb, h] = s0_ref[nb, h].T if state_kv else s0_ref[nb, h]

    lb = lb_ref[...]
    nw = nw_ref[...]
    t_row = lax.broadcasted_iota(jnp.int32, (chunk, chunk), 0)
    t_col = lax.broadcasted_iota(jnp.int32, (chunk, chunk), 1)
    causal = t_row >= t_col
    tri = jnp.where(causal, 1.0, 0.0).astype(BF16)
    tok = lax.broadcasted_iota(jnp.int32, (chunk, 1), 0)
    keep = (c * chunk + tok < l_valid) if l_valid < l_total else None

    gates = [_hgrn_gates(qb_ref[nb], fb_ref[nb], ib_ref[nb], gb_ref[nb], lb, keep) for nb in range(nbt)]
    decay = functools.reduce(jnp.maximum, [-jnp.sum(g[0], axis=0, keepdims=True) for g in gates])
    safe = jnp.max(decay) < HGRN_SAFE_DECAY

    def heads(h):
        return slice(h * HEAD_DIM, (h + 1) * HEAD_DIM)

    @pl.when(safe)
    def _():
        for nb in range(nbt):
            logf, kk, q, v, gate = gates[nb]
            hi, mid, lo = _split3(logf)
            b = _nn(tri, hi) + _nn(tri, mid) + _nn(tri, lo)
            b_end = b[chunk - 1:chunk, :]
            qd = (q * jnp.exp(b)).astype(BF16)
            ke = (kk * jnp.exp(-b)).astype(BF16)
            wd = (kk * jnp.exp(b_end - b)).astype(BF16)
            e_end = jnp.exp(b_end)
            vb = v.astype(BF16)
            for h in range(N_HEADS):
                hs = heads(h)
                a = jnp.where(causal, _nt(qd[:, hs], ke[:, hs]), 0.0).astype(BF16)
                st = st_sc[nb, h]
                o = _nn(a, vb[:, hs]) + _nt(qd[:, hs], st.astype(BF16))
                st_sc[nb, h] = st * e_end[:, hs] + _tn(vb[:, hs], wd[:, hs])
                y_ref[nb, :, hs] = (_rms(o) * nw[:, hs] * gate[:, hs]).astype(BF16)

    @pl.when(jnp.logical_not(safe))
    def _():
        def batch_body(nb, carry):
            logf, kk, q, v, gate = _hgrn_gates(qb_ref[nb], fb_ref[nb], ib_ref[nb], gb_ref[nb], lb, keep)
            for h in range(N_HEADS):
                hs = heads(h)
                lf, qh, vh = logf[:, hs], q[:, hs], v[:, hs]
                kh = kk[:, hs].astype(BF16)

                def token(t, carry_t):
                    st, o = carry_t
                    here = tok == t
                    f_row = jnp.exp(jnp.sum(jnp.where(here, lf, 0.0), axis=0, keepdims=True))
                    st = st * f_row + _tn(jnp.where(here, vh, 0.0).astype(BF16), kh)
                    o = o + _nt(jnp.where(here, qh, 0.0).astype(BF16), st.astype(BF16))
                    return st, o

                st, o = lax.fori_loop(0, chunk, token,
                                      (st_sc[nb, h], jnp.zeros((chunk, HEAD_DIM), F32)))
                st_sc[nb, h] = st
                y_ref[nb, :, hs] = (_rms(o) * nw[:, hs] * gate[:, hs]).astype(BF16)
            return carry

        lax.fori_loop(0, nbt, batch_body, 0)

    @pl.when(c == pl.num_programs(1) - 1)
    def _():
        for nb in range(nbt):
            for h in range(N_HEADS):
                s_ref[nb, h] = st_sc[nb, h].T if state_kv else st_sc[nb, h]


def _hgrn(z3, lb, norm_w, s0, *, chunk, nbt, l_valid, state_kv):
    nb, l_total, _ = z3.shape
    assert l_total % chunk == 0 and nb % nbt == 0

    def zspec(k):
        return pl.BlockSpec((nbt, chunk, D_HALF), lambda b, c: (b, c, k))

    sshape = (nbt, N_HEADS, HEAD_DIM, HEAD_DIM)
    sspec = pl.BlockSpec(sshape, lambda b, c: (b, 0, 0, 0))
    vec = pl.BlockSpec((1, D_HALF), lambda b, c: (0, 0))
    return pl.pallas_call(
        functools.partial(_hgrn_kernel, nbt=nbt, chunk=chunk, l_valid=l_valid, l_total=l_total,
                          state_kv=state_kv),
        grid=(nb // nbt, l_total // chunk),
        in_specs=[zspec(1), zspec(2), zspec(3), zspec(4), vec, vec, sspec],
        out_specs=[pl.BlockSpec((nbt, chunk, D_HALF), lambda b, c: (b, c, 0)), sspec],
        out_shape=[jax.ShapeDtypeStruct((nb, l_total, D_HALF), BF16),
                   jax.ShapeDtypeStruct(s0.shape, F32)],
        scratch_shapes=[pltpu.VMEM(sshape, F32)],
        compiler_params=_params("parallel", "arbitrary"),
        name="hgrn",
    )(z3, z3, z3, z3, lb, norm_w, s0)


def _outproj_router_kernel(hp_ref, hs_ref, yap_ref, yas_ref, ybp_ref, ybs_ref, anw_ref, wout_ref,
                           fnw_ref, wrh_ref, wrl_ref, br_ref,
                           h2_ref, xf_ref, slab_ref, slabt_ref, cnt_ref, *, tiles_p, tm):
    i = pl.program_id(0)
    is_p = i < tiles_p
    h = jnp.where(is_p, hp_ref[...], hs_ref[...])
    ya = jnp.where(is_p, yap_ref[...], yas_ref[...])
    yb = jnp.where(is_p, ybp_ref[...], ybs_ref[...])
    ya = (_rms(ya) * anw_ref[...]).astype(BF16)
    h2 = h + (_nn(ya, wout_ref[:D_HALF, :]) + _nn(yb, wout_ref[D_HALF:, :]))
    h2_ref[...] = h2
    xf = _rms(h2) * fnw_ref[...]
    xf_ref[...] = xf
    x_hi = xf.astype(BF16)
    x_lo = (xf - x_hi.astype(F32)).astype(BF16)
    w_hi = wrh_ref[...]
    logits = _nn(x_hi, w_hi) + _nn(x_lo, w_hi) + _nn(x_hi, wrl_ref[...]) + br_ref[...]
    lane = lax.broadcasted_iota(jnp.int32, logits.shape, 1)

    def first_max(x):
        m = jnp.max(x, axis=-1, keepdims=True)
        return m, jnp.min(jnp.where(x == m, lane, LANES), axis=-1, keepdims=True)

    gl = jnp.where(lane < N_GROUPS, logits, NEG)
    g_max, g_idx = first_max(gl)
    pg_top = 1.0 / jnp.sum(jnp.exp(gl - g_max), axis=-1, keepdims=True)
    lo = N_GROUPS + EXP_PER_GROUP * g_idx
    el = jnp.where((lane >= lo) & (lane < lo + EXP_PER_GROUP), logits, NEG)
    e1, i1 = first_max(el)
    e2, i2 = first_max(jnp.where(lane == i1, NEG, el))
    r = jnp.exp(e2 - e1)
    w1 = pg_top / (1.0 + r)
    w2 = w1 * r

    @pl.when(i == 0)
    def _():
        cnt_ref[...] = jnp.zeros_like(cnt_ref)

    onehot = jnp.where((lane == i1) | (lane == i2), 1.0, 0.0)
    t_row = lax.broadcasted_iota(jnp.int32, (tm, tm), 0)
    t_col = lax.broadcasted_iota(jnp.int32, (tm, tm), 1)
    before = jnp.where(t_row > t_col, 1.0, 0.0).astype(BF16)
    prefix = _nn(before, onehot.astype(BF16)) + cnt_ref[0:1, :]
    r1 = jnp.sum(jnp.where(lane == i1, prefix, 0.0), axis=-1, keepdims=True)
    r2 = jnp.sum(jnp.where(lane == i2, prefix, 0.0), axis=-1, keepdims=True)
    cnt_ref[...] = cnt_ref[...] + jnp.sum(onehot, axis=0, keepdims=True)
    cols = [(i1 - N_GROUPS).astype(F32), (i2 - N_GROUPS).astype(F32), w1, w2, r1, r2]
    slab = jnp.zeros(logits.shape, F32)
    for k, val in enumerate(cols):
        slab = jnp.where(lane == k, val, slab)
    slab_ref[...] = slab
    slabt_ref[...] = slab.T[:8, :]


def _outproj_router(h_p, h_s, ya_p, ya_s, yb_p, yb_s, a_norm_w, w_out, f_norm_w, wr_hi, wr_lo, b_r):
    n_p, n_s = h_p.shape[0], h_s.shape[0]
    tm = _pick_tile(n_s, 256)
    assert n_p % tm == 0
    tiles_p, tiles_s = n_p // tm, n_s // tm
    n = n_p + n_s
    pmap = lambda i: (jnp.minimum(i, tiles_p - 1), 0)
    smap = lambda i: (jnp.maximum(i - tiles_p, 0), 0)
    const = lambda i: (0, 0)
    row = lambda i: (i, 0)
    return pl.pallas_call(
        functools.partial(_outproj_router_kernel, tiles_p=tiles_p, tm=tm),
        grid=(tiles_p + tiles_s,),
        in_specs=[
            pl.BlockSpec((tm, D_MODEL), pmap), pl.BlockSpec((tm, D_MODEL), smap),
            pl.BlockSpec((tm, D_HALF), pmap), pl.BlockSpec((tm, D_HALF), smap),
            pl.BlockSpec((tm, D_HALF), pmap), pl.BlockSpec((tm, D_HALF), smap),
            pl.BlockSpec((1, D_HALF), const),
            pl.BlockSpec((D_MODEL, D_MODEL), const),
            pl.BlockSpec((1, D_MODEL), const),
            pl.BlockSpec((D_MODEL, LANES), const), pl.BlockSpec((D_MODEL, LANES), const),
            pl.BlockSpec((1, LANES), const),
        ],
        out_specs=[pl.BlockSpec((tm, D_MODEL), row), pl.BlockSpec((tm, D_MODEL), row),
                   pl.BlockSpec((tm, LANES), row), pl.BlockSpec((8, tm), lambda i: (0, i)),
                   pl.BlockSpec((8, LANES), const)],
        out_shape=[jax.ShapeDtypeStruct((n, D_MODEL), F32), jax.ShapeDtypeStruct((n, D_MODEL), F32),
                   jax.ShapeDtypeStruct((n, LANES), F32), jax.ShapeDtypeStruct((8, n), F32),
                   jax.ShapeDtypeStruct((8, LANES), F32)],
        compiler_params=_params("arbitrary"),
        name="outproj_router",
    )(h_p, h_s, ya_p, ya_s, yb_p, yb_s, a_norm_w, w_out, f_norm_w, wr_hi, wr_lo, b_r)


def _expert_kernel(te_ref, tv_ref, tr_ref, tok_ref, x_hbm, wg_ref, wu_ref, wd_ref, y_ref,
                   xbuf, sem, wg_sc, wu_sc, wd_sc):
    t = pl.program_id(0)
    n_tiles = pl.num_programs(0)
    slot = lax.rem(t, 2)

    def row_copy(tile, r, s):
        tok = tok_ref[tile * MOE_TILE + r]
        return pltpu.make_async_copy(x_hbm.at[tok], xbuf.at[s, pl.ds(r * X_SLABS, X_SLABS)], sem.at[s])

    def row_groups(tile):
        return (tr_ref[tile] + (GATHER_UNROLL - 1)) // GATHER_UNROLL

    def start_tile(tile, s):
        def body(i, carry):
            for j in range(GATHER_UNROLL):
                row_copy(tile, i * GATHER_UNROLL + j, s).start()
            return carry
        lax.fori_loop(0, row_groups(tile), body, 0)

    @pl.when(t == 0)
    def _():
        xbuf[...] = jnp.zeros_like(xbuf)
        start_tile(0, 0)

    nxt = jnp.minimum(t + 1, n_tiles - 1)

    @pl.when((t + 1 < n_tiles) & (tv_ref[nxt] == 1))
    def _():
        start_tile(nxt, 1 - slot)

    @pl.when((t == 0) | (te_ref[t] != te_ref[jnp.maximum(t - 1, 0)]))
    def _():
        wg_sc[...] = wg_ref[...].astype(BF16)
        wu_sc[...] = wu_ref[...].astype(BF16)
        wd_sc[...] = wd_ref[...].astype(BF16)

    @pl.when(tv_ref[t] == 1)
    def _():
        def body(i, carry):
            for j in range(GATHER_UNROLL):
                row_copy(t, i * GATHER_UNROLL + j, slot).wait()
            return carry
        lax.fori_loop(0, row_groups(t), body, 0)
        x = jnp.concatenate([xbuf[slot, pl.ds(s, MOE_TILE, stride=X_SLABS), :] for s in range(X_SLABS)],
                            axis=1).astype(BF16)
        hdn = (_silu(_nn(x, wg_sc[...])) * _nn(x, wu_sc[...])).astype(BF16)
        y_ref[...] = _nn(hdn, wd_sc[...])

    @pl.when(tv_ref[t] == 0)
    def _():
        y_ref[...] = jnp.zeros_like(y_ref)


def _expert_ffn(tile_expert, tile_valid, tile_rows, token_of_row, xf3, w_gate, w_up, w_down):
    n_rows = token_of_row.shape[0]
    grid_spec = pltpu.PrefetchScalarGridSpec(
        num_scalar_prefetch=4,
        grid=(n_rows // MOE_TILE,),
        in_specs=[
            pl.BlockSpec(memory_space=pl.ANY),
            pl.BlockSpec((None, D_MODEL, D_FF), lambda t, te, tv, tr, tok: (te[t], 0, 0)),
            pl.BlockSpec((None, D_MODEL, D_FF), lambda t, te, tv, tr, tok: (te[t], 0, 0)),
            pl.BlockSpec((None, D_FF, D_MODEL), lambda t, te, tv, tr, tok: (te[t], 0, 0)),
        ],
        out_specs=pl.BlockSpec((MOE_TILE, D_MODEL), lambda t, te, tv, tr, tok: (t, 0)),
        scratch_shapes=[pltpu.VMEM((2, MOE_TILE * X_SLABS, LANES), F32),
                        pltpu.SemaphoreType.DMA((2,)),
                        pltpu.VMEM((D_MODEL, D_FF), BF16), pltpu.VMEM((D_MODEL, D_FF), BF16),
                        pltpu.VMEM((D_FF, D_MODEL), BF16)],
    )
    return pl.pallas_call(
        _expert_kernel,
        grid_spec=grid_spec,
        out_shape=jax.ShapeDtypeStruct((n_rows, D_MODEL), F32),
        compiler_params=_params("arbitrary"),
        name="expert_ffn",
    )(tile_expert, tile_valid, tile_rows, token_of_row, xf3, w_gate, w_up, w_down)


def _final_kernel(h2_ref, y0_ref, y1_ref, slab_ref, nw_ref, op_ref, os_ref, *, tiles_p):
    i = pl.program_id(0)
    slab = slab_ref[...]
    h = h2_ref[...] + (slab[:, 2:3] * y0_ref[...] + slab[:, 3:4] * y1_ref[...])
    y = _rms(h) * nw_ref[...]

    @pl.when(i < tiles_p)
    def _():
        op_ref[...] = y

    @pl.when(i >= tiles_p)
    def _():
        os_ref[...] = y


def _final(h2, y_pair, slab, norm_w, n_p):
    n = h2.shape[0]
    n_s = n - n_p
    tm = _pick_tile(n_s, 256)
    tiles_p, tiles_s = n_p // tm, n_s // tm
    return pl.pallas_call(
        functools.partial(_final_kernel, tiles_p=tiles_p),
        grid=(tiles_p + tiles_s,),
        in_specs=[
            pl.BlockSpec((tm, D_MODEL), lambda i: (i, 0)),
            pl.BlockSpec((None, tm, D_MODEL), lambda i: (0, i, 0)),
            pl.BlockSpec((None, tm, D_MODEL), lambda i: (1, i, 0)),
            pl.BlockSpec((tm, LANES), lambda i: (i, 0)),
            pl.BlockSpec((1, D_MODEL), lambda i: (0, 0)),
        ],
        out_specs=[pl.BlockSpec((tm, D_MODEL), lambda i: (jnp.minimum(i, tiles_p - 1), 0)),
                   pl.BlockSpec((tm, D_MODEL), lambda i: (jnp.maximum(i - tiles_p, 0), 0))],
        out_shape=[jax.ShapeDtypeStruct((n_p, D_MODEL), F32), jax.ShapeDtypeStruct((n_s, D_MODEL), F32)],
        compiler_params=_params("arbitrary"),
        name="combine_final_norm",
    )(h2, y_pair, y_pair, slab, norm_w)


def _rope_table(pos):
    half = ROPE // 2
    inv_freq = jnp.power(ROPE_THETA, -jnp.arange(half, dtype=F32) / half)
    ang = pos.astype(F32)[:, None] * inv_freq[None, :]
    cos, sin = jnp.cos(ang), jnp.sin(ang)
    return jnp.concatenate([cos, cos, sin, sin], axis=-1)


def _rotate_half_cols(w):
    half = ROPE // 2
    return jnp.concatenate([-w[..., half:], w[..., :half]], axis=-1)


def kernel(x_prompt, x_sample, cache_ckv, cache_krope, state_hgrn, page_table, meta_tokens, attn_norm_w, w_in, q_norm_w, w_uq, kv_norm_w, w_uk, w_uv, mla_out_norm_w, hgrn_lb_logits, hgrn_out_norm_w, w_out, ffn_norm_w, w_router_group, b_router_group, w_router_expert, b_router_expert, w_gate, w_up, w_down, final_norm_w):
    n_batch, seq, _ = x_prompt.shape
    n_dec, n_new, _ = x_sample.shape
    depth = w_in.shape[0]
    assert depth == 1
    n_p, n_s = n_batch * seq, n_dec * n_new
    past = page_table.shape[1] * cache_ckv.shape[2]

    wit = jnp.swapaxes(w_in[0], 0, 1)
    c0 = Q_LORA + KV_LORA
    w_kr = wit[c0:c0 + ROPE]
    w_z = jnp.concatenate([wit[:c0], wit[c0 + ROPE:], w_kr, _rotate_half_cols(w_kr.T).T,
                           jnp.zeros((LANES, D_MODEL), F32)], axis=0).astype(BF16)
    uq = w_uq[0]
    uq_rope = uq[:, :, HEAD_DIM:]
    w_q = jnp.concatenate([uq[:, :, :HEAD_DIM].reshape(Q_LORA, D_HALF),
                           uq_rope.reshape(Q_LORA, N_HEADS * ROPE),
                           _rotate_half_cols(uq_rope).reshape(Q_LORA, N_HEADS * ROPE)], axis=1).astype(BF16)
    w_ukt = jnp.transpose(w_uk[0], (1, 2, 0)).astype(BF16)
    w_uvt = jnp.transpose(w_uv[0], (1, 0, 2)).astype(BF16)
    w_o = w_out[0].astype(BF16)
    w_r = jnp.concatenate([w_router_group[0], w_router_expert[0],
                           jnp.zeros((D_MODEL, LANES - N_GROUPS - N_EXPERTS), F32)], axis=1)
    wr_hi = w_r.astype(BF16)
    wr_lo = (w_r - wr_hi.astype(F32)).astype(BF16)
    b_r = jnp.concatenate([b_router_group[0], b_router_expert[0],
                           jnp.zeros((LANES - N_GROUPS - N_EXPERTS,), F32)])[None, :]
    lb =jax.nn.softmax(hgrn_lb_logits.astype(F32), axis=0)[0][None, :]
    a_nw, q_nw, kv_nw = attn_norm_w[0][None, :], q_norm_w[0][None, :], kv_norm_w[0][None, :]
    mo_nw, ho_nw, f_nw = mla_out_norm_w[0][None, :], hgrn_out_norm_w[0][None, :], ffn_norm_w[0][None, :]

    z_m = _in_proj(meta_tokens, a_nw, w_z)
    _, _, ckv_m, kr_m, ckvb_m, krb_m = _qkv_prep(z_m, _rope_table(jnp.arange(N_META)), q_nw, kv_nw, w_q, w_ukt)
    _, s_meta = _hgrn(z_m[None], lb, ho_nw, jnp.zeros((1, N_HEADS, HEAD_DIM, HEAD_DIM), F32),
                      chunk=N_META, nbt=1, l_valid=N_META, state_kv=False)
    meta_c = jnp.pad(ckvb_m, ((0, LANES - N_META), (0, 0)))
    meta_r = jnp.pad(krb_m, ((0, LANES - N_META), (0, 0)))

    xp = x_prompt.reshape(n_p, D_MODEL)
    z_p = _in_proj(xp, a_nw, w_z)
    cs_p = jnp.tile(_rope_table(N_META + jnp.arange(seq)), (n_batch, 1))
    ql_p, qr_p, ckv_p, kr_p, ckvb_p, krb_p = _qkv_prep(z_p, cs_p, q_nw, kv_nw, w_q, w_ukt)
    def keys_with_meta(m, p, d):
        return jnp.concatenate([jnp.broadcast_to(m[None], (n_batch, LANES, d)), p.reshape(n_batch, seq, d)], axis=1)

    ya_p = _attn_prompt(ql_p, qr_p, keys_with_meta(meta_c, ckvb_p, KV_LORA),
                        keys_with_meta(meta_r, krb_p, ROPE), w_uvt, n_batch)
    s0_p = jnp.broadcast_to(s_meta, (n_batch, N_HEADS, HEAD_DIM, HEAD_DIM))
    yb_p, st_p = _hgrn(z_p.reshape(n_batch, seq, Z_WIDTH), lb, ho_nw, s0_p,
                       chunk=min(HGRN_CHUNK, seq), nbt=n_batch, l_valid=seq, state_kv=False)
    st_p = jnp.swapaxes(st_p, -1, -2)

    xs = x_sample.reshape(n_s, D_MODEL)
    z_s = _in_proj(xs, a_nw, w_z)
    cs_s = jnp.tile(_rope_table(past + jnp.arange(n_new)), (n_dec, 1))
    ql_s, qr_s, ckv_s, kr_s, _, _ = _qkv_prep(z_s, cs_s, q_nw, kv_nw, w_q, w_ukt)

    def dec_rows(a):
        d = a.shape[-1]
        return jnp.transpose(a.reshape(N_HEADS, n_dec, n_new, d), (1, 0, 2, 3)).reshape(n_dec, N_HEADS * n_new, d)

    ya_s = _attn_decode(dec_rows(ql_s), dec_rows(qr_s), ckv_s.reshape(n_dec, n_new, KV_LORA),
                        kr_s.reshape(n_dec, n_new, ROPE), cache_ckv[0],
                        jnp.swapaxes(cache_krope[0], 1, 2), page_table, w_uvt)
    ya_s = jnp.transpose(ya_s.reshape(n_dec, N_HEADS, n_new, HEAD_DIM), (0, 2, 1, 3)).reshape(n_s, D_HALF)
    pad_new = -n_new % 8
    z_s3 = jnp.pad(z_s.reshape(n_dec, n_new, Z_WIDTH), ((0, 0), (0, pad_new), (0, 0)))
    yb_s, st_s = _hgrn(z_s3, lb, ho_nw, state_hgrn[0].astype(F32), chunk=n_new + pad_new,
                       nbt=math.gcd(n_dec, 4), l_valid=n_new, state_kv=True)
    yb_s = yb_s[:, :n_new].reshape(n_s, D_HALF)

    h2, xf, slab, slab_t, cnt = _outproj_router(xp, xs, ya_p, ya_s, yb_p.reshape(n_p, D_HALF), yb_s,
                                        mo_nw, w_o, f_nw, wr_hi, wr_lo, b_r)

    n = n_p + n_s
    routed = slab_t.astype(jnp.int32)
    counts = cnt[0, N_GROUPS:N_GROUPS + N_EXPERTS].astype(jnp.int32)
    padded = (counts + MOE_TILE - 1) // MOE_TILE * MOE_TILE
    ends = jnp.cumsum(padded)
    first = ends - padded
    pos = jnp.concatenate([first[routed[0]] + routed[4], first[routed[1]] + routed[5]])
    n_tiles = -(-2 * n // MOE_TILE) + N_EXPERTS
    starts = jnp.arange(n_tiles, dtype=jnp.int32) * MOE_TILE
    tile_valid = (starts < ends[-1]).astype(jnp.int32)
    last_valid = jnp.maximum(ends[-1] // MOE_TILE - 1, 0)
    tile_expert = jnp.sum((ends[None, :] <= jnp.minimum(starts, last_valid * MOE_TILE)[:, None])
                          .astype(jnp.int32), axis=1)
    tile_rows = jnp.clip(counts[tile_expert] - (starts - first[tile_expert]), 0, MOE_TILE) * tile_valid
    tokens = jnp.arange(n, dtype=jnp.int32)
    token_of_row = jnp.zeros((n_tiles * MOE_TILE,), jnp.int32).at[pos].set(jnp.concatenate([tokens, tokens]))
    y_sorted = _expert_ffn(tile_expert, tile_valid, tile_rows, token_of_row, xf.reshape(n, X_SLABS, LANES),
                           w_gate[0], w_up[0], w_down[0])
    y_slots = y_sorted.at[pos].get(mode="promise_in_bounds").reshape(2, n, D_MODEL)

    y_p, y_s = _final(h2, y_slots, slab, final_norm_w[None, :], n_p)

    def with_meta(m, p, d):
        return jnp.concatenate([jnp.broadcast_to(m[None], (n_batch, N_META, d)), p.reshape(n_batch, seq, d)], axis=1)[None]

    return (y_p.reshape(n_batch, seq, D_MODEL),
            y_s.reshape(n_dec, n_new, D_MODEL),
            with_meta(ckv_m, ckv_p, KV_LORA),
            with_meta(kr_m, kr_p, ROPE),
            st_p[None].astype(x_prompt.dtype),
            ckv_s.reshape(n_dec, n_new, KV_LORA)[None],
            kr_s.reshape(n_dec, n_new, ROPE)[None],
            st_s[None].astype(state_hgrn.dtype))
```

```python
import functools
import math

import jax
import jax.numpy as jnp
from jax import lax
from jax.experimental import pallas as pl
from jax.experimental.pallas import tpu as pltpu

F32 = jnp.float32
BF16 = jnp.bfloat16

D_MODEL = 2048
N_META = 16
N_HEADS = 8
HEAD_DIM = 128
ROPE = 64
Q_LORA = 512
KV_LORA = 512
D_HALF = N_HEADS * HEAD_DIM
ROPE_THETA = 10000.0
SM_SCALE = (HEAD_DIM + ROPE) ** -0.5
EPS = 1e-6
N_GROUPS = 4
EXP_PER_GROUP = 8
N_EXPERTS = N_GROUPS * EXP_PER_GROUP
D_FF = 512

Z_WIDTH = 5376
Z_TILE = 1792
Z_ROPE_BLOCK = 40

LANES = 128
VMEM_LIMIT = 56 * 1024 * 1024
NEG = -1e30
HGRN_SAFE_DECAY = 60.0
HGRN_CHUNK = 128
MOE_TILE = 256
X_SLABS = D_MODEL // LANES
GATHER_UNROLL = 8
PAGES_PER_CHUNK = 32
DECODE_SLOTS = 3


def _nn(a, b):
    return jnp.dot(a, b, preferred_element_type=F32)


def _nt(a, b):
    return lax.dot_general(a, b, (((1,), (1,)), ((), ())), preferred_element_type=F32)


def _tn(a, b):
    return lax.dot_general(a, b, (((0,), (0,)), ((), ())), preferred_element_type=F32)


def _split3(x):
    hi = x.astype(BF16)
    r1 = x - hi.astype(F32)
    mid = r1.astype(BF16)
    lo = (r1 - mid.astype(F32)).astype(BF16)
    return hi, mid, lo


def _rms(x):
    return x * lax.rsqrt(jnp.mean(x * x, axis=-1, keepdims=True) + EPS)


def _silu(x):
    return x * jax.nn.sigmoid(x)


def _pick_tile(n, pref):
    if n <= pref:
        return n
    for t in range(pref, 7, -1):
        if n % t == 0 and t % 8 == 0:
            return t
    raise ValueError(f"no tile for {n}")


def _params(*sem):
    return pltpu.CompilerParams(dimension_semantics=sem, vmem_limit_bytes=VMEM_LIMIT)


def _inproj_kernel(x_ref, nw_ref, w_ref, z_ref, xn_ref):
    @pl.when(pl.program_id(1) == 0)
    def _():
        xn_ref[...] = (_rms(x_ref[...]) * nw_ref[...]).astype(BF16)

    z_ref[...] = _nt(xn_ref[...], w_ref[...])


def _in_proj(x, norm_w, w_zt):
    n = x.shape[0]
    tm = _pick_tile(n, 1024)
    return pl.pallas_call(
        _inproj_kernel,
        grid=(n // tm, Z_WIDTH // Z_TILE),
        in_specs=[
            pl.BlockSpec((tm, D_MODEL), lambda i, j: (i, 0)),
            pl.BlockSpec((1, D_MODEL), lambda i, j: (0, 0)),
            pl.BlockSpec((Z_TILE, D_MODEL), lambda i, j: (j, 0)),
        ],
        out_specs=pl.BlockSpec((tm, Z_TILE), lambda i, j: (i, j)),
        out_shape=jax.ShapeDtypeStruct((n, Z_WIDTH), F32),
        scratch_shapes=[pltpu.VMEM((tm, D_MODEL), BF16)],
        compiler_params=_params("parallel", "arbitrary"),
        name="in_proj",
    )(x, norm_w, w_zt)


def _prep_kernel(zq_ref, zkv_ref, zkr_ref, cs_ref, qnw_ref, kvnw_ref, wq_ref, wuk_ref,
                 ql_ref, qr_ref, ckv_ref, kr_ref, ckvb_ref, krb_ref):
    cqn = (_rms(zq_ref[...]) * qnw_ref[...]).astype(BF16)
    q = _nn(cqn, wq_ref[...])
    cs = cs_ref[...]
    cos = cs[:, :ROPE]
    sin = cs[:, ROPE:]
    for h in range(N_HEADS):
        qn = q[:, h * HEAD_DIM:(h + 1) * HEAD_DIM].astype(BF16)
        ql_ref[h] = (_nn(qn, wuk_ref[h]) * SM_SCALE).astype(BF16)
        a = q[:, D_HALF + h * ROPE:D_HALF + (h + 1) * ROPE]
        b = q[:, D_HALF + N_HEADS * ROPE + h * ROPE:D_HALF + N_HEADS * ROPE + (h + 1) * ROPE]
        qr_ref[h] = ((a * cos + b * sin) * SM_SCALE).astype(BF16)
    ckv = _rms(zkv_ref[...]) * kvnw_ref[...]
    ckv_ref[...] = ckv
    ckvb_ref[...] = ckv.astype(BF16)
    prod = zkr_ref[...] * cs
    kr = prod[:, :ROPE] + prod[:, ROPE:]
    kr_ref[...] = kr
    krb_ref[...] = kr.astype(BF16)


def _qkv_prep(z, cs, q_norm_w, kv_norm_w, w_q, w_ukt):
    n = z.shape[0]
    tm = _pick_tile(n, 256)
    const2 = lambda i: (0, 0)
    return pl.pallas_call(
        _prep_kernel,
        grid=(n // tm,),
        in_specs=[
            pl.BlockSpec((tm, Q_LORA), lambda i: (i, 0)),
            pl.BlockSpec((tm, KV_LORA), lambda i: (i, 1)),
            pl.BlockSpec((tm, LANES), lambda i: (i, Z_ROPE_BLOCK)),
            pl.BlockSpec((tm, LANES), lambda i: (i, 0)),
            pl.BlockSpec((1, Q_LORA), const2),
            pl.BlockSpec((1, KV_LORA), const2),
            pl.BlockSpec((Q_LORA, 2 * D_HALF), const2),
            pl.BlockSpec((N_HEADS, HEAD_DIM, KV_LORA), lambda i: (0, 0, 0)),
        ],
        out_specs=[
            pl.BlockSpec((N_HEADS, tm, KV_LORA), lambda i: (0, i, 0)),
            pl.BlockSpec((N_HEADS, tm, ROPE), lambda i: (0, i, 0)),
            pl.BlockSpec((tm, KV_LORA), lambda i: (i, 0)),
            pl.BlockSpec((tm, ROPE), lambda i: (i, 0)),
            pl.BlockSpec((tm, KV_LORA), lambda i: (i, 0)),
            pl.BlockSpec((tm, ROPE), lambda i: (i, 0)),
        ],
        out_shape=[
            jax.ShapeDtypeStruct((N_HEADS, n, KV_LORA), BF16),
            jax.ShapeDtypeStruct((N_HEADS, n, ROPE), BF16),
            jax.ShapeDtypeStruct((n, KV_LORA), F32),
            jax.ShapeDtypeStruct((n, ROPE), F32),
            jax.ShapeDtypeStruct((n, KV_LORA), BF16),
            jax.ShapeDtypeStruct((n, ROPE), BF16),
        ],
        compiler_params=_params("parallel"),
        name="qkv_prep",
    )(z, z, z, cs, q_norm_w, kv_norm_w, w_q, w_ukt)


def _attn_prompt_kernel(ql_ref, qr_ref, kc_ref, kr_ref, wuv_ref, o_ref, *, tq, tk, n_blocks):
    qi = pl.program_id(1)
    rows = N_HEADS * tq
    ql = ql_ref[...].reshape(rows, KV_LORA)
    qr = qr_ref[...].reshape(rows, ROPE)
    needed = (qi * tq + (tq - 1)) // tk + 1

    def prefix(nblk):
        width = LANES + nblk * tk
        kc = kc_ref[0:width, :]
        s = _nt(ql, kc) + _nt(qr, kr_ref[0:width, :])
        s_meta = s[:, :LANES]
        s_meta = jnp.where(lax.broadcasted_iota(jnp.int32, s_meta.shape, 1) < N_META, s_meta, NEG)
        s_last = s[:, width - tk:]
        row = lax.broadcasted_iota(jnp.int32, s_last.shape, 0)
        col = lax.broadcasted_iota(jnp.int32, s_last.shape, 1)
        s_last = jnp.where((nblk - 1) * tk + col <= qi * tq + (row & (tq - 1)), s_last, NEG)
        middle = [s[:, LANES:width - tk]] if nblk > 1 else []
        s = jnp.concatenate([s_meta] + middle + [s_last], axis=1)
        p = jnp.exp(s - jnp.max(s, axis=-1, keepdims=True))
        o = (_nn(p.astype(BF16), kc) / jnp.sum(p, axis=-1, keepdims=True)).astype(BF16)
        for h in range(N_HEADS):
            o_ref[:, h * HEAD_DIM:(h + 1) * HEAD_DIM] = _nn(o[h * tq:(h + 1) * tq], wuv_ref[h])

    for nblk in range(1, n_blocks + 1):
        pl.when(needed == nblk)(functools.partial(prefix, nblk))


def _attn_prompt(ql, qr, keys_c, keys_r, w_uvt, n_batch):
    seq = keys_c.shape[1] - LANES
    n = n_batch * seq
    tq = 128
    tk = min(512, seq)
    assert seq % tq == 0 and seq % tk == 0 and tq & (tq - 1) == 0
    qb = seq // tq
    return pl.pallas_call(
        functools.partial(_attn_prompt_kernel, tq=tq, tk=tk, n_blocks=seq // tk),
        grid=(n_batch, qb),
        in_specs=[
            pl.BlockSpec((N_HEADS, tq, KV_LORA), lambda b, i: (0, b * qb + i, 0)),
            pl.BlockSpec((N_HEADS, tq, ROPE), lambda b, i: (0, b * qb + i, 0)),
            pl.BlockSpec((None, LANES + seq, KV_LORA), lambda b, i: (b, 0, 0)),
            pl.BlockSpec((None, LANES + seq, ROPE), lambda b, i: (b, 0, 0)),
            pl.BlockSpec((N_HEADS, KV_LORA, HEAD_DIM), lambda b, i: (0, 0, 0)),
        ],
        out_specs=pl.BlockSpec((tq, D_HALF), lambda b, i: (b * qb + i, 0)),
        out_shape=jax.ShapeDtypeStruct((n, D_HALF), F32),
        compiler_params=_params("parallel", "parallel"),
        name="mla_prompt",
    )(ql, qr, keys_c, keys_r, w_uvt)


def _attn_decode_kernel(pt_ref, ql_ref, qr_ref, nc_ref, nr_ref, cc_hbm, cr_hbm, wuv_ref, o_ref,
                        cbuf, rbuf, sem, kc_sc, kr_sc, m_sc, l_sc, acc_sc,
                        *, chunk, n_chunks, n_batch, n_new, page):
    b = pl.program_id(0)
    total = n_batch * n_chunks
    ahead = DECODE_SLOTS - 1
    rows = N_HEADS * n_new
    ql = ql_ref[...]
    qr = qr_ref[...]

    def chunk_copies(g):
        bb = lax.div(g, jnp.int32(n_chunks))
        first_page = (g - bb * n_chunks) * chunk
        slot = lax.rem(g, jnp.int32(DECODE_SLOTS))
        copies = []
        for k in range(chunk):
            pg = pt_ref[bb, first_page + k]
            copies.append(pltpu.make_async_copy(cc_hbm.at[pg], cbuf.at[slot, k], sem.at[slot]))
            copies.append(pltpu.make_async_copy(cr_hbm.at[pg], rbuf.at[slot, k], sem.at[slot]))
        return copies

    @pl.when(b == 0)
    def _():
        for g in range(min(ahead, total)):
            for cp in chunk_copies(jnp.int32(g)):
                cp.start()

    qlf = ql.astype(F32)
    qrf = qr.astype(F32)
    nc = nc_ref[...]
    nr = nr_ref[...]
    tok = lax.rem(lax.broadcasted_iota(jnp.int32, (rows, 1), 0), n_new)
    cols = []
    for t in range(n_new):
        sc = (jnp.sum(qlf * nc[t:t + 1, :], axis=-1, keepdims=True)
              + jnp.sum(qrf * nr[t:t + 1, :], axis=-1, keepdims=True))
        cols.append(jnp.where(tok >= t, sc, NEG))
    m0 = functools.reduce(jnp.maximum, cols)
    ps = [jnp.exp(c - m0) for c in cols]
    m_sc[...] = m0
    l_sc[...] = functools.reduce(jnp.add, ps)
    acc_sc[...] = functools.reduce(jnp.add, [p * nc[t:t + 1, :] for t, p in enumerate(ps)])

    def chunk_body(c, carry):
        g = b * n_chunks + c
        slot = lax.rem(g, jnp.int32(DECODE_SLOTS))
        for cp in chunk_copies(g):
            cp.wait()

        @pl.when(g + ahead < total)
        def _():
            for cp in chunk_copies(g + ahead):
                cp.start()

        for k in range(chunk):
            kc_sc[k * page:(k + 1) * page, :] = cbuf[slot, k].astype(BF16)
            kr_sc[:, k * page:(k + 1) * page] = rbuf[slot, k].astype(BF16)
        kc = kc_sc[...]
        s = _nt(ql, kc) + _nn(qr, kr_sc[...])
        m_prev = m_sc[...]
        m_new = jnp.maximum(m_prev, jnp.max(s, axis=-1, keepdims=True))
        alpha = jnp.exp(m_prev - m_new)
        p = jnp.exp(s - m_new)
        l_sc[...] = alpha * l_sc[...] + jnp.sum(p, axis=-1, keepdims=True)
        acc_sc[...] = alpha * acc_sc[...] + _nn(p.astype(BF16), kc)
        m_sc[...] = m_new
        return carry

    lax.fori_loop(0, n_chunks, chunk_body, 0)

    o = (acc_sc[...] / l_sc[...]).astype(BF16)
    head = lax.broadcasted_iota(jnp.int32, (rows, 1), 0) // n_new
    y = jnp.zeros((rows, HEAD_DIM), F32)
    for h in range(N_HEADS):
        y = jnp.where(head == h, _nn(o, wuv_ref[h]), y)
    o_ref[...] = y


def _attn_decode(ql, qr, new_c, new_r, cache_c, cache_r, page_table, w_uvt):
    nb, rows, _ = ql.shape
    n_new = rows // N_HEADS
    total_pages = page_table.shape[1]
    chunk = min(PAGES_PER_CHUNK, total_pages)
    n_chunks = total_pages // chunk
    assert total_pages % chunk == 0
    page = cache_c.shape[1]

    row3 = lambda b, pt: (b, 0, 0)
    grid_spec = pltpu.PrefetchScalarGridSpec(
        num_scalar_prefetch=1,
        grid=(nb,),
        in_specs=[
            pl.BlockSpec((None, rows, KV_LORA), row3),
            pl.BlockSpec((None, rows, ROPE), row3),
            pl.BlockSpec((None, n_new, KV_LORA), row3),
            pl.BlockSpec((None, n_new, ROPE), row3),
            pl.BlockSpec(memory_space=pl.ANY),
            pl.BlockSpec(memory_space=pl.ANY),
            pl.BlockSpec((N_HEADS, KV_LORA, HEAD_DIM), lambda b, pt: (0, 0, 0)),
        ],
        out_specs=pl.BlockSpec((None, rows, HEAD_DIM), row3),
        scratch_shapes=[pltpu.VMEM((DECODE_SLOTS, chunk, page, KV_LORA), F32),
                        pltpu.VMEM((DECODE_SLOTS, chunk, ROPE, page), F32),
                        pltpu.SemaphoreType.DMA((DECODE_SLOTS,)),
                        pltpu.VMEM((chunk * page, KV_LORA), BF16),
                        pltpu.VMEM((ROPE, chunk * page), BF16),
                        pltpu.VMEM((rows, 1), F32), pltpu.VMEM((rows, 1), F32),
                        pltpu.VMEM((rows, KV_LORA), F32)],
    )
    return pl.pallas_call(
        functools.partial(_attn_decode_kernel, chunk=chunk, n_chunks=n_chunks, n_batch=nb, n_new=n_new,
                          page=page),
        grid_spec=grid_spec,
        out_shape=jax.ShapeDtypeStruct((nb, rows, HEAD_DIM), F32),
        compiler_params=_params("arbitrary"),
        name="mla_decode",
    )(page_table, ql, qr, new_c, new_r, cache_c, cache_r, w_uvt)


def _hgrn_gates(qb, fb, ib, gb, lb, keep):
    logf = jnp.log(lb + (1.0 - lb) * jax.nn.sigmoid(fb))
    kk = (1.0 - lb) * jax.nn.sigmoid(-fb)
    if keep is not None:
        logf = jnp.where(keep, logf, 0.0)
        kk = jnp.where(keep, kk, 0.0)
    return logf, kk, _silu(qb), ib, _silu(gb)


def _hgrn_kernel(qb_ref, fb_ref, ib_ref, gb_ref, lb_ref, nw_ref, s0_ref, y_ref, s_ref, st_sc,
                 *, nbt, chunk, l_valid, l_total, state_kv):
    c = pl.program_id(1)

    @pl.when(c == 0)
    def _():
        for nb in range(nbt):
            for h in range(N_HEADS):
                st_sc[nb, h] = s0_ref[nb, h].T if state_kv else s0_ref[nb, h]

    lb = lb_ref[...]
    nw = nw_ref[...]
    t_row = lax.broadcasted_iota(jnp.int32, (chunk, chunk), 0)
    t_col = lax.broadcasted_iota(jnp.int32, (chunk, chunk), 1)
    causal = t_row >= t_col
    tri = jnp.where(causal, 1.0, 0.0).astype(BF16)
    tok = lax.broadcasted_iota(jnp.int32, (chunk, 1), 0)
    keep = (c * chunk + tok < l_valid) if l_valid < l_total else None

    gates = [_hgrn_gates(qb_ref[nb], fb_ref[nb], ib_ref[nb], gb_ref[nb], lb, keep) for nb in range(nbt)]
    decay = functools.reduce(jnp.maximum, [-jnp.sum(g[0], axis=0, keepdims=True) for g in gates])
    safe = jnp.max(decay) < HGRN_SAFE_DECAY

    def heads(h):
        return slice(h * HEAD_DIM, (h + 1) * HEAD_DIM)

    @pl.when(safe)
    def _():
        for nb in range(nbt):
            logf, kk, q, v, gate = gates[nb]
            hi, mid, lo = _split3(logf)
            b = _nn(tri, hi) + _nn(tri, mid) + _nn(tri, lo)
            b_end = b[chunk - 1:chunk, :]
            qd = (q * jnp.exp(b)).astype(BF16)
            ke = (kk * jnp.exp(-b)).astype(BF16)
            wd = (kk * jnp.exp(b_end - b)).astype(BF16)
            e_end = jnp.exp(b_end)
            vb = v.astype(BF16)
            for h in range(N_HEADS):
                hs = heads(h)
                a = jnp.where(causal, _nt(qd[:, hs], ke[:, hs]), 0.0).astype(BF16)
                st = st_sc[nb, h]
                o = _nn(a, vb[:, hs]) + _nt(qd[:, hs], st.astype(BF16))
                st_sc[nb, h] = st * e_end[:, hs] + _tn(vb[:, hs], wd[:, hs])
                y_ref[nb, :, hs] = (_rms(o) * nw[:, hs] * gate[:, hs]).astype(BF16)

    @pl.when(jnp.logical_not(safe))
    def _():
        def batch_body(nb, carry):
            logf, kk, q, v, gate = _hgrn_gates(qb_ref[nb], fb_ref[nb], ib_ref[nb], gb_ref[nb], lb, keep)
            for h in range(N_HEADS):
                hs = heads(h)
                lf, qh, vh = logf[:, hs], q[:, hs], v[:, hs]
                kh = kk[:, hs].astype(BF16)

                def token(t, carry_t):
                    st, o = carry_t
                    here = tok == t
                    f_row = jnp.exp(jnp.sum(jnp.where(here, lf, 0.0), axis=0, keepdims=True))
                    st = st * f_row + _tn(jnp.where(here, vh, 0.0).astype(BF16), kh)
                    o = o + _nt(jnp.where(here, qh, 0.0).astype(BF16), st.astype(BF16))
                    return st, o

                st, o = lax.fori_loop(0, chunk, token,
                                      (st_sc[nb, h], jnp.zeros((chunk, HEAD_DIM), F32)))
                st_sc[nb, h] = st
                y_ref[nb, :, hs] = (_rms(o) * nw[:, hs] * gate[:, hs]).astype(BF16)
            return carry

        lax.fori_loop(0, nbt, batch_body, 0)

    @pl.when(c == pl.num_programs(1) - 1)
    def _():
        for nb in range(nbt):
            for h in range(N_HEADS):
                s_ref[nb, h] = st_sc[nb, h].T if state_kv else st_sc[nb, h]


def _hgrn(z3, lb, norm_w, s0, *, chunk, nbt, l_valid, state_kv):
    nb, l_total, _ = z3.shape
    assert l_total % chunk == 0 and nb % nbt == 0

    def zspec(k):
        return pl.BlockSpec((nbt, chunk, D_HALF), lambda b, c: (b, c, k))

    sshape = (nbt, N_HEADS, HEAD_DIM, HEAD_DIM)
    sspec = pl.BlockSpec(sshape, lambda b, c: (b, 0, 0, 0))
    vec = pl.BlockSpec((1, D_HALF), lambda b, c: (0, 0))
    return pl.pallas_call(
        functools.partial(_hgrn_kernel, nbt=nbt, chunk=chunk, l_valid=l_valid, l_total=l_total,
                          state_kv=state_kv),
        grid=(nb // nbt, l_total // chunk),
        in_specs=[zspec(1), zspec(2), zspec(3), zspec(4), vec, vec, sspec],
        out_specs=[pl.BlockSpec((nbt, chunk, D_HALF), lambda b, c: (b, c, 0)), sspec],
        out_shape=[jax.ShapeDtypeStruct((nb, l_total, D_HALF), BF16),
                   jax.ShapeDtypeStruct(s0.shape, F32)],
        scratch_shapes=[pltpu.VMEM(sshape, F32)],
        compiler_params=_params("parallel", "arbitrary"),
        name="hgrn",
    )(z3, z3, z3, z3, lb, norm_w, s0)


def _outproj_router_kernel(hp_ref, hs_ref, yap_ref, yas_ref, ybp_ref, ybs_ref, anw_ref, wout_ref,
                           fnw_ref, wrh_ref, wrl_ref, br_ref,
                           h2_ref, xf_ref, slab_ref, slabt_ref, cnt_ref, *, tiles_p, tm):
    i = pl.program_id(0)
    is_p = i < tiles_p
    h = jnp.where(is_p, hp_ref[...], hs_ref[...])
    ya = jnp.where(is_p, yap_ref[...], yas_ref[...])
    yb = jnp.where(is_p, ybp_ref[...], ybs_ref[...])
    ya = (_rms(ya) * anw_ref[...]).astype(BF16)
    h2 = h + (_nn(ya, wout_ref[:D_HALF, :]) + _nn(yb, wout_ref[D_HALF:, :]))
    h2_ref[...] = h2
    xf = _rms(h2) * fnw_ref[...]
    xf_ref[...] = xf
    x_hi = xf.astype(BF16)
    x_lo = (xf - x_hi.astype(F32)).astype(BF16)
    w_hi = wrh_ref[...]
    logits = _nn(x_hi, w_hi) + _nn(x_lo, w_hi) + _nn(x_hi, wrl_ref[...]) + br_ref[...]
    lane = lax.broadcasted_iota(jnp.int32, logits.shape, 1)

    def first_max(x):
        m = jnp.max(x, axis=-1, keepdims=True)
        return m, jnp.min(jnp.where(x == m, lane, LANES), axis=-1, keepdims=True)

    gl = jnp.where(lane < N_GROUPS, logits, NEG)
    g_max, g_idx = first_max(gl)
    pg_top = 1.0 / jnp.sum(jnp.exp(gl - g_max), axis=-1, keepdims=True)
    lo = N_GROUPS + EXP_PER_GROUP * g_idx
    el = jnp.where((lane >= lo) & (lane < lo + EXP_PER_GROUP), logits, NEG)
    e1, i1 = first_max(el)
    e2, i2 = first_max(jnp.where(lane == i1, NEG, el))
    r = jnp.exp(e2 - e1)
    w1 = pg_top / (1.0 + r)
    w2 = w1 * r

    @pl.when(i == 0)
    def _():
        cnt_ref[...] = jnp.zeros_like(cnt_ref)

    onehot = jnp.where((lane == i1) | (lane == i2), 1.0, 0.0)
    t_row = lax.broadcasted_iota(jnp.int32, (tm, tm), 0)
    t_col = lax.broadcasted_iota(jnp.int32, (tm, tm), 1)
    before = jnp.where(t_row > t_col, 1.0, 0.0).astype(BF16)
    prefix = _nn(before, onehot.astype(BF16)) + cnt_ref[0:1, :]
    r1 = jnp.sum(jnp.where(lane == i1, prefix, 0.0), axis=-1, keepdims=True)
    r2 = jnp.sum(jnp.where(lane == i2, prefix, 0.0), axis=-1, keepdims=True)
    cnt_ref[...] = cnt_ref[...] + jnp.sum(onehot, axis=0, keepdims=True)
    cols = [(i1 - N_GROUPS).astype(F32), (i2 - N_GROUPS).astype(F32), w1, w2, r1, r2]
    slab = jnp.zeros(logits.shape, F32)
    for k, val in enumerate(cols):
        slab = jnp.where(lane == k, val, slab)
    slab_ref[...] = slab
    slabt_ref[...] = slab.T[:8, :]


def _outproj_router(h_p, h_s, ya_p, ya_s, yb_p, yb_s, a_norm_w, w_out, f_norm_w, wr_hi, wr_lo, b_r):
    n_p, n_s = h_p.shape[0], h_s.shape[0]
    tm = _pick_tile(n_s, 256)
    assert n_p % tm == 0
    tiles_p, tiles_s = n_p // tm, n_s // tm
    n = n_p + n_s
    pmap = lambda i: (jnp.minimum(i, tiles_p - 1), 0)
    smap = lambda i: (jnp.maximum(i - tiles_p, 0), 0)
    const = lambda i: (0, 0)
    row = lambda i: (i, 0)
    return pl.pallas_call(
        functools.partial(_outproj_router_kernel, tiles_p=tiles_p, tm=tm),
        grid=(tiles_p + tiles_s,),
        in_specs=[
            pl.BlockSpec((tm, D_MODEL), pmap), pl.BlockSpec((tm, D_MODEL), smap),
            pl.BlockSpec((tm, D_HALF), pmap), pl.BlockSpec((tm, D_HALF), smap),
            pl.BlockSpec((tm, D_HALF), pmap), pl.BlockSpec((tm, D_HALF), smap),
            pl.BlockSpec((1, D_HALF), const),
            pl.BlockSpec((D_MODEL, D_MODEL), const),
            pl.BlockSpec((1, D_MODEL), const),
            pl.BlockSpec((D_MODEL, LANES), const), pl.BlockSpec((D_MODEL, LANES), const),
            pl.BlockSpec((1, LANES), const),
        ],
        out_specs=[pl.BlockSpec((tm, D_MODEL), row), pl.BlockSpec((tm, D_MODEL), row),
                   pl.BlockSpec((tm, LANES), row), pl.BlockSpec((8, tm), lambda i: (0, i)),
                   pl.BlockSpec((8, LANES), const)],
        out_shape=[jax.ShapeDtypeStruct((n, D_MODEL), F32), jax.ShapeDtypeStruct((n, D_MODEL), F32),
                   jax.ShapeDtypeStruct((n, LANES), F32), jax.ShapeDtypeStruct((8, n), F32),
                   jax.ShapeDtypeStruct((8, LANES), F32)],
        compiler_params=_params("arbitrary"),
        name="outproj_router",
    )(h_p, h_s, ya_p, ya_s, yb_p, yb_s, a_norm_w, w_out, f_norm_w, wr_hi, wr_lo, b_r)


def _expert_kernel(te_ref, tv_ref, tr_ref, tok_ref, x_hbm, wg_ref, wu_ref, wd_ref, y_ref,
                   xbuf, sem, wg_sc, wu_sc, wd_sc):
    t = pl.program_id(0)
    n_tiles = pl.num_programs(0)
    slot = lax.rem(t, 2)

    def row_copy(tile, r, s):
        tok = tok_ref[tile * MOE_TILE + r]
        return pltpu.make_async_copy(x_hbm.at[tok], xbuf.at[s, pl.ds(r * X_SLABS, X_SLABS)], sem.at[s])

    def row_groups(tile):
        return (tr_ref[tile] + (GATHER_UNROLL - 1)) // GATHER_UNROLL

    def start_tile(tile, s):
        def body(i, carry):
            for j in range(GATHER_UNROLL):
                row_copy(tile, i * GATHER_UNROLL + j, s).start()
            return carry
        lax.fori_loop(0, row_groups(tile), body, 0)

    @pl.when(t == 0)
    def _():
        xbuf[...] = jnp.zeros_like(xbuf)
        start_tile(0, 0)

    nxt = jnp.minimum(t + 1, n_tiles - 1)

    @pl.when((t + 1 < n_tiles) & (tv_ref[nxt] == 1))
    def _():
        start_tile(nxt, 1 - slot)

    @pl.when((t == 0) | (te_ref[t] != te_ref[jnp.maximum(t - 1, 0)]))
    def _():
        wg_sc[...] = wg_ref[...].astype(BF16)
        wu_sc[...] = wu_ref[...].astype(BF16)
        wd_sc[...] = wd_ref[...].astype(BF16)

    @pl.when(tv_ref[t] == 1)
    def _():
        def body(i, carry):
            for j in range(GATHER_UNROLL):
                row_copy(t, i * GATHER_UNROLL + j, slot).wait()
            return carry
        lax.fori_loop(0, row_groups(t), body, 0)
        x = jnp.concatenate([xbuf[slot, pl.ds(s, MOE_TILE, stride=X_SLABS), :] for s in range(X_SLABS)],
                            axis=1).astype(BF16)
        hdn = (_silu(_nn(x, wg_sc[...])) * _nn(x, wu_sc[...])).astype(BF16)
        y_ref[...] = _nn(hdn, wd_sc[...])

    @pl.when(tv_ref[t] == 0)
    def _():
        y_ref[...] = jnp.zeros_like(y_ref)


def _expert_ffn(tile_expert, tile_valid, tile_rows, token_of_row, xf3, w_gate, w_up, w_down):
    n_rows = token_of_row.shape[0]
    grid_spec = pltpu.PrefetchScalarGridSpec(
        num_scalar_prefetch=4,
        grid=(n_rows // MOE_TILE,),
        in_specs=[
            pl.BlockSpec(memory_space=pl.ANY),
            pl.BlockSpec((None, D_MODEL, D_FF), lambda t, te, tv, tr, tok: (te[t], 0, 0)),
            pl.BlockSpec((None, D_MODEL, D_FF), lambda t, te, tv, tr, tok: (te[t], 0, 0)),
            pl.BlockSpec((None, D_FF, D_MODEL), lambda t, te, tv, tr, tok: (te[t], 0, 0)),
        ],
        out_specs=pl.BlockSpec((MOE_TILE, D_MODEL), lambda t, te, tv, tr, tok: (t, 0)),
        scratch_shapes=[pltpu.VMEM((2, MOE_TILE * X_SLABS, LANES), F32),
                        pltpu.SemaphoreType.DMA((2,)),
                        pltpu.VMEM((D_MODEL, D_FF), BF16), pltpu.VMEM((D_MODEL, D_FF), BF16),
                        pltpu.VMEM((D_FF, D_MODEL), BF16)],
    )
    return pl.pallas_call(
        _expert_kernel,
        grid_spec=grid_spec,
        out_shape=jax.ShapeDtypeStruct((n_rows, D_MODEL), F32),
        compiler_params=_params("arbitrary"),
        name="expert_ffn",
    )(tile_expert, tile_valid, tile_rows, token_of_row, xf3, w_gate, w_up, w_down)


def _final_kernel(h2_ref, y0_ref, y1_ref, slab_ref, nw_ref, op_ref, os_ref, *, tiles_p):
    i = pl.program_id(0)
    slab = slab_ref[...]
    h = h2_ref[...] + (slab[:, 2:3] * y0_ref[...] + slab[:, 3:4] * y1_ref[...])
    y = _rms(h) * nw_ref[...]

    @pl.when(i < tiles_p)
    def _():
        op_ref[...] = y

    @pl.when(i >= tiles_p)
    def _():
        os_ref[...] = y


def _final(h2, y_pair, slab, norm_w, n_p):
    n = h2.shape[0]
    n_s = n - n_p
    tm = _pick_tile(n_s, 256)
    tiles_p, tiles_s = n_p // tm, n_s // tm
    return pl.pallas_call(
        functools.partial(_final_kernel, tiles_p=tiles_p),
        grid=(tiles_p + tiles_s,),
        in_specs=[
            pl.BlockSpec((tm, D_MODEL), lambda i: (i, 0)),
            pl.BlockSpec((None, tm, D_MODEL), lambda i: (0, i, 0)),
            pl.BlockSpec((None, tm, D_MODEL), lambda i: (1, i, 0)),
            pl.BlockSpec((tm, LANES), lambda i: (i, 0)),
            pl.BlockSpec((1, D_MODEL), lambda i: (0, 0)),
        ],
        out_specs=[pl.BlockSpec((tm, D_MODEL), lambda i: (jnp.minimum(i, tiles_p - 1), 0)),
                   pl.BlockSpec((tm, D_MODEL), lambda i: (jnp.maximum(i - tiles_p, 0), 0))],
        out_shape=[jax.ShapeDtypeStruct((n_p, D_MODEL), F32), jax.ShapeDtypeStruct((n_s, D_MODEL), F32)],
        compiler_params=_params("arbitrary"),
        name="combine_final_norm",
    )(h2, y_pair, y_pair, slab, norm_w)


def _rope_table(pos):
    half = ROPE // 2
    inv_freq = jnp.power(ROPE_THETA, -jnp.arange(half, dtype=F32) / half)
    ang = pos.astype(F32)[:, None] * inv_freq[None, :]
    cos, sin = jnp.cos(ang), jnp.sin(ang)
    return jnp.concatenate([cos, cos, sin, sin], axis=-1)


def _rotate_half_cols(w):
    half = ROPE // 2
    return jnp.concatenate([-w[..., half:], w[..., :half]], axis=-1)


def kernel(x_prompt, x_sample, cache_ckv, cache_krope, state_hgrn, page_table, meta_tokens, attn_norm_w, w_in, q_norm_w, w_uq, kv_norm_w, w_uk, w_uv, mla_out_norm_w, hgrn_lb_logits, hgrn_out_norm_w, w_out, ffn_norm_w, w_router_group, b_router_group, w_router_expert, b_router_expert, w_gate, w_up, w_down, final_norm_w):
    n_batch, seq, _ = x_prompt.shape
    n_dec, n_new, _ = x_sample.shape
    depth = w_in.shape[0]
    assert depth == 1
    n_p, n_s = n_batch * seq, n_dec * n_new
    past = page_table.shape[1] * cache_ckv.shape[2]

    wit = jnp.swapaxes(w_in[0], 0, 1)
    c0 = Q_LORA + KV_LORA
    w_kr = wit[c0:c0 + ROPE]
    w_z = jnp.concatenate([wit[:c0], wit[c0 + ROPE:], w_kr, _rotate_half_cols(w_kr.T).T,
                           jnp.zeros((LANES, D_MODEL), F32)], axis=0).astype(BF16)
    uq = w_uq[0]
    uq_rope = uq[:, :, HEAD_DIM:]
    w_q = jnp.concatenate([uq[:, :, :HEAD_DIM].reshape(Q_LORA, D_HALF),
                           uq_rope.reshape(Q_LORA, N_HEADS * ROPE),
                           _rotate_half_cols(uq_rope).reshape(Q_LORA, N_HEADS * ROPE)], axis=1).astype(BF16)
    w_ukt = jnp.transpose(w_uk[0], (1, 2, 0)).astype(BF16)
    w_uvt = jnp.transpose(w_uv[0], (1, 0, 2)).astype(BF16)
    w_o = w_out[0].astype(BF16)
    w_r = jnp.concatenate([w_router_group[0], w_router_expert[0],
                           jnp.zeros((D_MODEL, LANES - N_GROUPS - N_EXPERTS), F32)], axis=1)
    wr_hi = w_r.astype(BF16)
    wr_lo = (w_r - wr_hi.astype(F32)).astype(BF16)
    b_r = jnp.concatenate([b_router_group[0], b_router_expert[0],
                           jnp.zeros((LANES - N_GROUPS - N_EXPERTS,), F32)])[None, :]
    lb =jax.nn.softmax(hgrn_lb_logits.astype(F32), axis=0)[0][None, :]
    a_nw, q_nw, kv_nw = attn_norm_w[0][None, :], q_norm_w[0][None, :], kv_norm_w[0][None, :]
    mo_nw, ho_nw, f_nw = mla_out_norm_w[0][None, :], hgrn_out_norm_w[0][None, :], ffn_norm_w[0][None, :]

    z_m = _in_proj(meta_tokens, a_nw, w_z)
    _, _, ckv_m, kr_m, ckvb_m, krb_m = _qkv_prep(z_m, _rope_table(jnp.arange(N_META)), q_nw, kv_nw, w_q, w_ukt)
    _, s_meta = _hgrn(z_m[None], lb, ho_nw, jnp.zeros((1, N_HEADS, HEAD_DIM, HEAD_DIM), F32),
                      chunk=N_META, nbt=1, l_valid=N_META, state_kv=False)
    meta_c = jnp.pad(ckvb_m, ((0, LANES - N_META), (0, 0)))
    meta_r = jnp.pad(krb_m, ((0, LANES - N_META), (0, 0)))

    xp = x_prompt.reshape(n_p, D_MODEL)
    z_p = _in_proj(xp, a_nw, w_z)
    cs_p = jnp.tile(_rope_table(N_META + jnp.arange(seq)), (n_batch, 1))
    ql_p, qr_p, ckv_p, kr_p, ckvb_p, krb_p = _qkv_prep(z_p, cs_p, q_nw, kv_nw, w_q, w_ukt)
    def keys_with_meta(m, p, d):
        return jnp.concatenate([jnp.broadcast_to(m[None], (n_batch, LANES, d)), p.reshape(n_batch, seq, d)], axis=1)

    ya_p = _attn_prompt(ql_p, qr_p, keys_with_meta(meta_c, ckvb_p, KV_LORA),
                        keys_with_meta(meta_r, krb_p, ROPE), w_uvt, n_batch)
    s0_p = jnp.broadcast_to(s_meta, (n_batch, N_HEADS, HEAD_DIM, HEAD_DIM))
    yb_p, st_p = _hgrn(z_p.reshape(n_batch, seq, Z_WIDTH), lb, ho_nw, s0_p,
                       chunk=min(HGRN_CHUNK, seq), nbt=n_batch, l_valid=seq, state_kv=False)
    st_p = jnp.swapaxes(st_p, -1, -2)

    xs = x_sample.reshape(n_s, D_MODEL)
    z_s = _in_proj(xs, a_nw, w_z)
    cs_s = jnp.tile(_rope_table(past + jnp.arange(n_new)), (n_dec, 1))
    ql_s, qr_s, ckv_s, kr_s, _, _ = _qkv_prep(z_s, cs_s, q_nw, kv_nw, w_q, w_ukt)

    def dec_rows(a):
        d = a.shape[-1]
        return jnp.transpose(a.reshape(N_HEADS, n_dec, n_new, d), (1, 0, 2, 3)).reshape(n_dec, N_HEADS * n_new, d)

    ya_s = _attn_decode(dec_rows(ql_s), dec_rows(qr_s), ckv_s.reshape(n_dec, n_new, KV_LORA),
                        kr_s.reshape(n_dec, n_new, ROPE), cache_ckv[0],
                        jnp.swapaxes(cache_krope[0], 1, 2), page_table, w_uvt)
    ya_s = jnp.transpose(ya_s.reshape(n_dec, N_HEADS, n_new, HEAD_DIM), (0, 2, 1, 3)).reshape(n_s, D_HALF)
    pad_new = -n_new % 8
    z_s3 = jnp.pad(z_s.reshape(n_dec, n_new, Z_WIDTH), ((0, 0), (0, pad_new), (0, 0)))
    yb_s, st_s = _hgrn(z_s3, lb, ho_nw, state_hgrn[0].astype(F32), chunk=n_new + pad_new,
                       nbt=math.gcd(n_dec, 4), l_valid=n_new, state_kv=True)
    yb_s = yb_s[:, :n_new].reshape(n_s, D_HALF)

    h2, xf, slab, slab_t, cnt = _outproj_router(xp, xs, ya_p, ya_s, yb_p.reshape(n_p, D_HALF), yb_s,
                                        mo_nw, w_o, f_nw, wr_hi, wr_lo, b_r)

    n = n_p + n_s
    routed = slab_t.astype(jnp.int32)
    counts = cnt[0, N_GROUPS:N_GROUPS + N_EXPERTS].astype(jnp.int32)
    padded = (counts + MOE_TILE - 1) // MOE_TILE * MOE_TILE
    ends = jnp.cumsum(padded)
    first = ends - padded
    pos = jnp.concatenate([first[routed[0]] + routed[4], first[routed[1]] + routed[5]])
    n_tiles = -(-2 * n // MOE_TILE) + N_EXPERTS
    starts = jnp.arange(n_tiles, dtype=jnp.int32) * MOE_TILE
    tile_valid = (starts < ends[-1]).astype(jnp.int32)
    last_valid = jnp.maximum(ends[-1] // MOE_TILE - 1, 0)
    tile_expert = jnp.sum((ends[None, :] <= jnp.minimum(starts, last_valid * MOE_TILE)[:, None])
                          .astype(jnp.int32), axis=1)
    tile_rows = jnp.clip(counts[tile_expert] - (starts - first[tile_expert]), 0, MOE_TILE) * tile_valid
    tokens = jnp.arange(n, dtype=jnp.int32)
    token_of_row = jnp.zeros((n_tiles * MOE_TILE,), jnp.int32).at[pos].set(
        jnp.concatenate([tokens, tokens]), unique_indices=True, mode="promise_in_bounds")
    y_sorted = _expert_ffn(tile_expert, tile_valid, tile_rows, token_of_row, xf.reshape(n, X_SLABS, LANES),
                           w_gate[0], w_up[0], w_down[0])
    y_slots = y_sorted.at[pos].get(unique_indices=True, mode="promise_in_bounds").reshape(2, n, D_MODEL)

    y_p, y_s = _final(h2, y_slots, slab, final_norm_w[None, :], n_p)

    def with_meta(m, p, d):
        return jnp.concatenate([jnp.broadcast_to(m[None], (n_batch, N_META, d)), p.reshape(n_batch, seq, d)], axis=1)[None]

    return (y_p.reshape(n_batch, seq, D_MODEL),
            y_s.reshape(n_dec, n_new, D_MODEL),
            with_meta(ckv_m, ckv_p, KV_LORA),
            with_meta(kr_m, kr_p, ROPE),
            st_p[None].astype(x_prompt.dtype),
            ckv_s.reshape(n_dec, n_new, KV_LORA)[None],
            kr_s.reshape(n_dec, n_new, ROPE)[None],
            st_s[None].astype(state_hgrn.dtype))
```
